```python
import math
import jax, jax.numpy as jnp
from jax import lax
import numpy as np

D_MODEL = 1024
BATCH = 2
SEQ = 8192
DEPTH = 2

RET_WIDTH = D_MODEL // 2
RET_HEAD_DIM = 64
RET_HEADS = RET_WIDTH // RET_HEAD_DIM
DIFF_WIDTH = D_MODEL - RET_WIDTH
DIFF_HEAD_DIM = 64
DIFF_V_DIM = 2 * DIFF_HEAD_DIM
DIFF_HEADS = DIFF_WIDTH // DIFF_V_DIM
MIX_WIDTH = RET_WIDTH + DIFF_WIDTH
D_IN_PROJ = 4 * RET_WIDTH + 3 * DIFF_WIDTH
D_FF = -(-8 * D_MODEL // (3 * 256)) * 256
CHUNK = 128
Q_BLOCK = 128
EPS = 1e-6

kernel_name = "hybrid_retention_diffattn_encoder"


def rms_norm(x, g):
    xf = x.astype(jnp.float32)
    y = xf * lax.rsqrt(jnp.mean(xf * xf, axis=-1, keepdims=True) + EPS)
    return (y * g.astype(jnp.float32)).astype(x.dtype)


def alibi_slopes(n):
    return 2.0 ** (-8.0 * jnp.arange(1, n + 1, dtype=jnp.float32) / n)


def retention_bidir(q, k, v, lg_f, lg_b):
    B, S, H, D = q.shape
    N = S // CHUNK
    qc = q.astype(jnp.float32).reshape(B, N, CHUNK, H, D)
    kc = k.astype(jnp.float32).reshape(B, N, CHUNK, H, D)
    vc = v.astype(jnp.float32).reshape(B, N, CHUNK, H, D)
    pos = jnp.arange(CHUNK, dtype=jnp.float32)
    dist = pos[:, None] - pos[None, :]
    mask_f = jnp.where(dist >= 0, jnp.exp(lg_f[:, None, None] * jnp.maximum(dist, 0.0)), 0.0)
    mask_b = jnp.where(dist < 0, jnp.exp(lg_b[:, None, None] * jnp.maximum(-dist, 0.0)), 0.0)
    dmask = mask_f + mask_b
    scores = jnp.einsum('bnthd,bnshd->bnhts', qc, kc) * dmask[None, None]
    intra = jnp.einsum('bnhts,bnshd->bnthd', scores, vc)

    wk_f = jnp.exp(lg_f[None, :] * (CHUNK - 1 - pos)[:, None])
    wq_f = jnp.exp(lg_f[None, :] * (pos + 1)[:, None])
    kv_f = jnp.einsum('bnshd,bnshe->nbhde', kc * wk_f[None, None, :, :, None], vc)
    dc_f = jnp.exp(lg_f * CHUNK)[None, :, None, None]
    wk_b = jnp.exp(lg_b[None, :] * pos[:, None])
    wq_b = jnp.exp(lg_b[None, :] * (CHUNK - pos)[:, None])
    kv_b = jnp.einsum('bnshd,bnshe->nbhde', kc * wk_b[None, None, :, :, None], vc)
    dc_b = jnp.exp(lg_b * CHUNK)[None, :, None, None]

    zeros = jnp.zeros((B, H, D, D), jnp.float32)

    def fwd_step(R, kv):
        return dc_f * R + kv, R

    def bwd_step(R, kv):
        return dc_b * R + kv, R

    _, R_f = lax.scan(fwd_step, zeros, kv_f)
    _, R_b = lax.scan(bwd_step, zeros, kv_b, reverse=True)
    cross_f = jnp.einsum('bnthd,nbhde->bnthe', qc * wq_f[None, None, :, :, None], R_f)
    cross_b = jnp.einsum('bnthd,nbhde->bnthe', qc * wq_b[None, None, :, :, None], R_b)
    out = (intra + cross_f + cross_b).reshape(B, S, H, D)
    return out.astype(q.dtype)


def diff_attention(q, k, v, lam, slopes):
    B, S, H, _, Dh = q.shape
    NB = S // Q_BLOCK
    q_blocks = q.reshape(B, NB, Q_BLOCK, H, 2, Dh).transpose(1, 0, 2, 3, 4, 5)
    starts = jnp.arange(NB, dtype=jnp.int32) * Q_BLOCK
    kpos = jnp.arange(S, dtype=jnp.int32)
    vf = v.astype(jnp.float32)

    def block(args):
        qb, start = args
        s = jnp.einsum('bqhmd,bkhmd->bhmqk', qb, k).astype(jnp.float32)
        qpos = start + jnp.arange(Q_BLOCK, dtype=jnp.int32)
        dist = jnp.abs(qpos[:, None] - kpos[None, :]).astype(jnp.float32)
        s = s - slopes[None, :, None, None, None] * dist[None, None, None]
        p = jax.nn.softmax(s, axis=-1)
        a = p[:, :, 0] - lam * p[:, :, 1]
        return jnp.einsum('bhqk,bkhe->bqhe', a, vf)

    o = lax.map(block, (q_blocks, starts))
    return o.transpose(1, 0, 2, 3, 4).reshape(B, S, H, DIFF_V_DIM).astype(q.dtype)


def setup_inputs(seed: int = 0) -> dict:
    key = jax.random.key(seed)
    ks = jax.random.split(key, 20)
    f32 = jnp.float32
    nrm = lambda k, shape, scale: jax.random.normal(k, shape, f32) * scale
    heads = jnp.arange(RET_HEADS, dtype=f32)
    base = jnp.log(-jnp.log1p(-(2.0 ** (-5.0 - heads))))
    return {
        "x": jax.random.normal(ks[0], (BATCH, SEQ, D_MODEL), f32),
        "attn_norm_g": 1.0 + nrm(ks[1], (DEPTH, D_MODEL), 0.02),
        "w_in": nrm(ks[2], (DEPTH, D_MODEL, D_IN_PROJ), D_MODEL ** -0.5),
        "ret_decay_fwd": base[None, :] + nrm(ks[3], (DEPTH, RET_HEADS), 0.1),
        "ret_decay_bwd": base[None, :] + nrm(ks[4], (DEPTH, RET_HEADS), 0.1),
        "ret_norm_g": 1.0 + nrm(ks[5], (DEPTH, RET_HEAD_DIM), 0.02),
        "dq_norm_g": 1.0 + nrm(ks[6], (DEPTH, DIFF_HEAD_DIM), 0.02),
        "dk_norm_g": 1.0 + nrm(ks[7], (DEPTH, DIFF_HEAD_DIM), 0.02),
        "lambda_q1": nrm(ks[8], (DEPTH, DIFF_HEAD_DIM), 0.1),
        "lambda_k1": nrm(ks[9], (DEPTH, DIFF_HEAD_DIM), 0.1),
        "lambda_q2": nrm(ks[10], (DEPTH, DIFF_HEAD_DIM), 0.1),
        "lambda_k2": nrm(ks[11], (DEPTH, DIFF_HEAD_DIM), 0.1),
        "diff_norm_g": 1.0 + nrm(ks[12], (DEPTH, DIFF_V_DIM), 0.02),
        "w_out": nrm(ks[13], (DEPTH, MIX_WIDTH, D_MODEL), MIX_WIDTH ** -0.5),
        "ffn_norm_g": 1.0 + nrm(ks[14], (DEPTH, D_MODEL), 0.02),
        "w_gate": nrm(ks[15], (DEPTH, D_MODEL, D_FF), D_MODEL ** -0.5),
        "w_up": nrm(ks[16], (DEPTH, D_MODEL, D_FF), D_MODEL ** -0.5),
        "w_down": nrm(ks[17], (DEPTH, D_FF, D_MODEL), D_FF ** -0.5),
    }


def reference(x, attn_norm_g, w_in, ret_decay_fwd, ret_decay_bwd, ret_norm_g,
              dq_norm_g, dk_norm_g, lambda_q1, lambda_k1, lambda_q2, lambda_k2,
              diff_norm_g, w_out, ffn_norm_g, w_gate, w_up, w_down):
    B, S, _ = x.shape
    slopes = alibi_slopes(DIFF_HEADS)
    split_at = [RET_WIDTH, 2 * RET_WIDTH, 3 * RET_WIDTH, 4 * RET_WIDTH,
                4 * RET_WIDTH + DIFF_WIDTH, 4 * RET_WIDTH + 2 * DIFF_WIDTH]
    for l in range(DEPTH):
        lam_init = 0.8 - 0.6 * math.exp(-0.3 * l)
        h = rms_norm(x, attn_norm_g[l])
        proj = h @ w_in[l]
        rq, rk, rv, rg, dq, dk, dv = jnp.split(proj, split_at, axis=-1)

        shp = (B, S, RET_HEADS, RET_HEAD_DIM)
        lg_f = -jnp.exp(ret_decay_fwd[l].astype(jnp.float32))
        lg_b = -jnp.exp(ret_decay_bwd[l].astype(jnp.float32))
        ret = retention_bidir(rq.reshape(shp), rk.reshape(shp) * RET_HEAD_DIM ** -0.5,
                              rv.reshape(shp), lg_f, lg_b)
        ret = rms_norm(ret, ret_norm_g[l]) * jax.nn.silu(rg.reshape(shp))
        ret = ret.reshape(B, S, RET_WIDTH)

        qk_shp = (B, S, DIFF_HEADS, 2, DIFF_HEAD_DIM)
        dqn = rms_norm(dq.reshape(qk_shp), dq_norm_g[l]) * DIFF_HEAD_DIM ** -0.5
        dkn = rms_norm(dk.reshape(qk_shp), dk_norm_g[l])
        lam = (jnp.exp(jnp.sum(lambda_q1[l].astype(jnp.float32) * lambda_k1[l].astype(jnp.float32)))
               - jnp.exp(jnp.sum(lambda_q2[l].astype(jnp.float32) * lambda_k2[l].astype(jnp.float32)))
               + lam_init)
        da = diff_attention(dqn, dkn, dv.reshape(B, S, DIFF_HEADS, DIFF_V_DIM), lam, slopes)
        da = (rms_norm(da, diff_norm_g[l]) * (1.0 - lam_init)).reshape(B, S, DIFF_WIDTH)

        x = x + jnp.concatenate([ret, da], axis=-1) @ w_out[l]

        h = rms_norm(x, ffn_norm_g[l])
        x = x + (jax.nn.silu(h @ w_gate[l]) * (h @ w_up[l])) @ w_down[l]
    return x
```

```python
import functools
import math

import numpy as np
import jax
import jax.numpy as jnp
from jax import lax
from jax.experimental import pallas as pl
from jax.experimental.pallas import tpu as pltpu

EPS = 1e-6
LOG2E = 1.4426950408889634

RET_HEAD_DIM = 64
DIFF_HEAD_DIM = 64
DIFF_V_DIM = 128
LANES = 128
VMEM_LIMIT = 56 * 1024 * 1024

BF16 = jnp.bfloat16
F32 = jnp.float32


def _cparams(sem):
    return pltpu.CompilerParams(dimension_semantics=sem, vmem_limit_bytes=VMEM_LIMIT)


def _in_proj_kernel(x_ref, g_ref, w_ref, o_ref):
    x = x_ref[...]
    y = x * lax.rsqrt(jnp.mean(x * x, axis=-1, keepdims=True) + EPS) * g_ref[...]
    o_ref[...] = jnp.dot(y.astype(BF16), w_ref[...], preferred_element_type=F32).astype(BF16)


def _in_proj(x2, g, w, tm):
    m, d = x2.shape
    n = w.shape[1]
    return pl.pallas_call(
        _in_proj_kernel,
        grid=(m // tm,),
        in_specs=[
            pl.BlockSpec((tm, d), lambda i: (i, 0)),
            pl.BlockSpec((1, d), lambda i: (0, 0)),
            pl.BlockSpec((d, n), lambda i: (0, 0)),
        ],
        out_specs=pl.BlockSpec((tm, n), lambda i: (i, 0)),
        out_shape=jax.ShapeDtypeStruct((m, n), BF16),
        compiler_params=_cparams(("parallel",)),
        name="in_proj",
    )(x2, g, w)


N_AUG = 12
POS_RADIX = 32


def _pos_digits(pos):
    hi = (pos // POS_RADIX) * POS_RADIX
    lo = pos % POS_RADIX
    return hi.astype(F32), lo.astype(F32)


def _dprep_kernel(q_ref, k_ref, v_ref, gq_ref, gk_ref, cp_ref, qt_ref, ka_ref, vt_ref, *, n_heads):
    ts = q_ref.shape[0]
    ng = 2 * n_heads
    hd = DIFF_HEAD_DIM

    def head_norm_t(ref, g_col, scale):
        xt = ref[...].astype(F32).T.reshape(ng, hd, ts)
        ms = jnp.mean(xt * xt, axis=1, keepdims=True)
        return xt * lax.rsqrt(ms + EPS) * (g_col[None] * scale)

    qn = head_norm_t(q_ref, gq_ref[...], DIFF_HEAD_DIM ** -0.5 * LOG2E)
    kn = head_norm_t(k_ref, gk_ref[...], 1.0)

    pos = lax.broadcasted_iota(jnp.int32, (1, ts), 1)
    hi, lo = _pos_digits(pos)
    r = lax.broadcasted_iota(jnp.int32, (hd, 1), 0)
    digit = jnp.where(r % 2 == 0, hi, lo)
    k_digits = jnp.where(r < N_AUG // 2, digit, 0.0)
    q_digits = jnp.where((r >= N_AUG // 2) & (r < N_AUG), -digit, 0.0)
    for g in range(ng):
        q_aug = jnp.concatenate([qn[g], q_digits + cp_ref[g // 2, 0]], axis=0)
        k_aug = jnp.concatenate([kn[g], k_digits + cp_ref[g // 2, 1]], axis=0)
        qt_ref[g] = q_aug.astype(BF16)
        ka_ref[g] = k_aug.T.astype(BF16)
    vt_ref[...] = v_ref[...].astype(F32).T.reshape(n_heads, DIFF_V_DIM, ts).astype(BF16)


def _dprep(proj3, gq, gk, cpieces, n_heads, ts, col0):
    b, s, _ = proj3.shape
    ng = 2 * n_heads
    assert n_heads * DIFF_V_DIM == 512
    cq, ck, cv = col0 // 512, col0 // 512 + 1, col0 // 512 + 2
    return pl.pallas_call(
        functools.partial(_dprep_kernel, n_heads=n_heads),
        grid=(b, s // ts),
        in_specs=[
            pl.BlockSpec((None, ts, 512), lambda i, j: (i, j, cq)),
            pl.BlockSpec((None, ts, 512), lambda i, j: (i, j, ck)),
            pl.BlockSpec((None, ts, 512), lambda i, j: (i, j, cv)),
            pl.BlockSpec((DIFF_HEAD_DIM, 1), lambda i, j: (0, 0)),
            pl.BlockSpec((DIFF_HEAD_DIM, 1), lambda i, j: (0, 0)),
            pl.BlockSpec((n_heads, 2, DIFF_HEAD_DIM, 1), lambda i, j: (0, 0, 0, 0)),
        ],
        out_specs=[
            pl.BlockSpec((None, ng, LANES, ts), lambda i, j: (i, 0, 0, j)),
            pl.BlockSpec((None, ng, ts, LANES), lambda i, j: (i, 0, j, 0)),
            pl.BlockSpec((None, n_heads, DIFF_V_DIM, ts), lambda i, j: (i, 0, 0, j)),
        ],
        out_shape=[
            jax.ShapeDtypeStruct((b, ng, LANES, s), BF16),
            jax.ShapeDtypeStruct((b, ng, s, LANES), BF16),
            jax.ShapeDtypeStruct((b, n_heads, DIFF_V_DIM, s), BF16),
        ],
        compiler_params=_cparams(("parallel", "parallel")),
        name="dprep",
    )(proj3, proj3, proj3, gq, gk, cpieces)


NEG_BIG = -1e30


def _diffattn_kernel(lam_ref, qt_ref, ka_ref, vt_ref, gd_ref, o_ref,
                     qs_ref, m_ref, l_ref, acc_ref, *, tq, lam_init, slopes_l2):
    h = pl.program_id(1)
    qi = pl.program_id(2)
    s_len = ka_ref.shape[1]
    nk = s_len // tq
    slope = jnp.float32(0.0)
    for hh, sv in enumerate(slopes_l2):
        slope = jnp.where(h == hh, jnp.float32(sv), slope)

    row = lax.broadcasted_iota(jnp.int32, (LANES, 1), 0)
    sign = jnp.where(row < DIFF_HEAD_DIM, 1.0, -1.0).astype(BF16)
    for mp in range(2):
        q = qt_ref[mp]
        qs_ref[0, mp] = q
        qs_ref[1, mp] = q * sign

    m_ref[...] = jnp.full(m_ref.shape, NEG_BIG, F32)
    l_ref[...] = jnp.zeros(l_ref.shape, F32)
    acc_ref[...] = jnp.zeros(acc_ref.shape, F32)

    def block(j, diag):
        start = pl.multiple_of(j * tq, tq)
        above = (j > qi).astype(jnp.int32)
        off = slope * (jnp.abs(qi - j) * tq).astype(F32)
        if diag:
            ii = lax.broadcasted_iota(jnp.int32, (tq, tq), 0)
            jj = lax.broadcasted_iota(jnp.int32, (tq, tq), 1)
            corr = (2.0 * slope) * jnp.maximum(ii - jj, 0).astype(F32)
        ps = []
        for mp in range(2):
            k = ka_ref[mp, pl.ds(start, tq), :]
            s = jnp.dot(k, qs_ref[above, mp], preferred_element_type=F32)
            if diag:
                s = s - corr
            m_old = m_ref[mp]
            m_new = jnp.maximum(m_old, jnp.max(s, axis=0, keepdims=True) - off)
            p = jnp.exp2(s - (m_new + off))
            alpha = jnp.exp2(m_old - m_new)
            l_ref[mp] = alpha * l_ref[mp] + jnp.sum(p, axis=0, keepdims=True)
            m_ref[mp] = m_new
            acc_ref[mp] = acc_ref[mp] * alpha
            ps.append(p.astype(BF16))
        pcat = jnp.concatenate(ps, axis=1)
        vt = vt_ref[:, pl.ds(start, tq)]
        pv = jnp.dot(vt, pcat, preferred_element_type=F32)
        acc_ref[0] += pv[:, :tq]
        acc_ref[1] += pv[:, tq:]

    def generic(it, c):
        block(it + (it >= qi).astype(jnp.int32), diag=False)
        return c

    lax.fori_loop(0, nk - 1, generic, 0)
    block(qi, diag=True)

    la = lam_ref[...]
    lam = (jnp.exp(jnp.sum(la[0:1] * la[1:2], axis=-1, keepdims=True))
           - jnp.exp(jnp.sum(la[2:3] * la[3:4], axis=-1, keepdims=True)) + lam_init)
    o1 = acc_ref[0] / l_ref[0]
    o2 = acc_ref[1] / l_ref[1]
    da = o1 - lam * o2
    y = da * lax.rsqrt(jnp.mean(da * da, axis=0, keepdims=True) + EPS)
    y = y * (gd_ref[...] * (1.0 - lam_init))
    o_ref[...] = y.T.astype(BF16)


def _diffattn(lam_params, qt, ka, vt, gd_col, tq, lam_init, slopes_l2):
    b, ng, _, s = qt.shape
    n_heads = ng // 2
    kern = functools.partial(_diffattn_kernel, tq=tq, lam_init=lam_init, slopes_l2=slopes_l2)
    return pl.pallas_call(
        kern,
        grid=(b, n_heads, s // tq),
        in_specs=[
            pl.BlockSpec((4, DIFF_HEAD_DIM), lambda i, h, q: (0, 0)),
            pl.BlockSpec((None, 2, LANES, tq), lambda i, h, q: (i, h, 0, q)),
            pl.BlockSpec((None, 2, s, LANES), lambda i, h, q: (i, h, 0, 0)),
            pl.BlockSpec((None, None, DIFF_V_DIM, s), lambda i, h, q: (i, h, 0, 0)),
            pl.BlockSpec((DIFF_V_DIM, 1), lambda i, h, q: (0, 0)),
        ],
        out_specs=pl.BlockSpec((None, tq, DIFF_V_DIM), lambda i, h, q: (i, q, h)),
        out_shape=jax.ShapeDtypeStruct((b, s, n_heads * DIFF_V_DIM), BF16),
        scratch_shapes=[
            pltpu.VMEM((2, 2, LANES, tq), BF16),
            pltpu.VMEM((2, 1, tq), F32),
            pltpu.VMEM((2, 1, tq), F32),
            pltpu.VMEM((2, DIFF_V_DIM, tq), F32),
        ],
        compiler_params=_cparams(("parallel", "parallel", "parallel")),
        name="diffattn",
    )(lam_params, qt, ka, vt, gd_col)


def _retention_kernel(dec_ref, q_ref, k_ref, v_ref, g_ref, gn_ref, o_ref, rf_ref, *, chunk):
    j = pl.program_id(1)
    s_len = q_ref.shape[0]
    n_chunks = s_len // chunk
    hd = RET_HEAD_DIM
    c = chunk

    lane = lax.broadcasted_iota(jnp.int32, (1, LANES), 1)
    rowi = lax.broadcasted_iota(jnp.int32, (LANES, 1), 0)
    first_l = lane < hd
    first_r = rowi < hd

    def per_lane(d):
        return -jnp.exp(jnp.where(first_l, dec_ref[d, 2 * j], dec_ref[d, 2 * j + 1]))

    def per_row(d):
        return -jnp.exp(jnp.where(first_r, dec_ref[d, 2 * j], dec_ref[d, 2 * j + 1]))

    lgf_l, lgb_l = per_lane(0), per_lane(1)
    lgf_r, lgb_r = per_row(0), per_row(1)

    pos_r = lax.broadcasted_iota(jnp.int32, (c, 1), 0).astype(F32)
    wk_f = jnp.exp(lgf_l * (c - 1.0 - pos_r))
    wq_f = jnp.exp(lgf_l * (pos_r + 1.0))
    wk_b = jnp.exp(lgb_l * pos_r)
    wq_b = jnp.exp(lgb_l * (c - pos_r))
    dc_f = jnp.exp(lgf_r * float(c))
    dc_b = jnp.exp(lgb_r * float(c))

    tt = lax.broadcasted_iota(jnp.int32, (c, c), 0)
    ss = lax.broadcasted_iota(jnp.int32, (c, c), 1)
    dist = (tt - ss).astype(F32)

    def dmask(hh):
        lf = -jnp.exp(jnp.full((1, 1), dec_ref[0, 2 * j + hh], F32))
        lb = -jnp.exp(jnp.full((1, 1), dec_ref[1, 2 * j + hh], F32))
        return jnp.where(dist >= 0, jnp.exp(lf * jnp.maximum(dist, 0.0)),
                         jnp.exp(lb * jnp.maximum(-dist, 0.0)))

    d0, d1 = dmask(0), dmask(1)
    bd = (first_r == first_l).astype(F32)
    m0 = first_l.astype(F32)
    m1 = 1.0 - m0
    kscale = RET_HEAD_DIM ** -0.5

    def fwd(n, rf):
        r0 = pl.multiple_of(n * c, c)
        rf_ref[n] = rf.astype(BF16)
        kc = k_ref[pl.ds(r0, c), :].astype(F32) * kscale
        vc = v_ref[pl.ds(r0, c), :]
        kv = lax.dot_general((kc * wk_f).astype(BF16), vc, (((0,), (0,)), ((), ())),
                             preferred_element_type=F32)
        return dc_f * rf + kv * bd

    lax.fori_loop(0, n_chunks, fwd, jnp.zeros((LANES, LANES), F32))

    gn = gn_ref[...]

    def bwd(it, rb):
        n = n_chunks - 1 - it
        r0 = pl.multiple_of(n * c, c)
        qc = q_ref[pl.ds(r0, c), :].astype(F32)
        kc = k_ref[pl.ds(r0, c), :].astype(F32) * kscale
        vc = v_ref[pl.ds(r0, c), :].astype(F32)
        kcb = kc.astype(BF16)
        nt = (((1,), (1,)), ((), ()))
        s0 = lax.dot_general((qc * m0).astype(BF16), kcb, nt, preferred_element_type=F32) * d0
        s1 = lax.dot_general((qc * m1).astype(BF16), kcb, nt, preferred_element_type=F32) * d1
        a = jnp.concatenate([s0, s1, qc * wq_f, qc * wq_b], axis=1).astype(BF16)
        bm = jnp.concatenate([(vc * m0).astype(BF16), (vc * m1).astype(BF16),
                              rf_ref[n], rb.astype(BF16)], axis=0)
        o = jnp.dot(a, bm, preferred_element_type=F32)
        o2 = o * o
        ms0 = jnp.sum(o2 * m0, axis=-1, keepdims=True)
        ms1 = jnp.sum(o2 * m1, axis=-1, keepdims=True)
        ms = jnp.where(first_l, ms0, ms1) * (1.0 / hd)
        y = o * lax.rsqrt(ms + EPS) * gn
        gate = g_ref[pl.ds(r0, c), :].astype(F32)
        y = y * (gate * jax.nn.sigmoid(gate))
        o_ref[pl.ds(r0, c), :] = y.astype(BF16)
        kv = lax.dot_general((kc * wk_b).astype(BF16), vc.astype(BF16), (((0,), (0,)), ((), ())),
                             preferred_element_type=F32)
        return dc_b * rb + kv * bd

    lax.fori_loop(0, n_chunks, bwd, jnp.zeros((LANES, LANES), F32))


def _retention(dec, proj3, gn_lanes, ret_width, chunk):
    b, s, _ = proj3.shape
    npair = ret_width // LANES
    assert chunk == LANES
    blk = lambda o: pl.BlockSpec((None, s, LANES), lambda i, j, o=o: (i, 0, o * npair + j))
    return pl.pallas_call(
        functools.partial(_retention_kernel, chunk=chunk),
        grid=(b, npair),
        in_specs=[
            pl.BlockSpec(memory_space=pltpu.SMEM),
            blk(0), blk(1), blk(2), blk(3),
            pl.BlockSpec((1, LANES), lambda i, j: (0, 0)),
        ],
        out_specs=pl.BlockSpec((None, s, LANES), lambda i, j: (i, 0, j)),
        out_shape=jax.ShapeDtypeStruct((b, s, ret_width), BF16),
        scratch_shapes=[pltpu.VMEM((s // chunk, LANES, LANES), BF16)],
        compiler_params=_cparams(("parallel", "parallel")),
        name="retention",
    )(dec, proj3, proj3, proj3, proj3, gn_lanes)


def _post_kernel(x_ref, r_ref, a_ref, wo_ref, g_ref, wg_ref, wu_ref, wd_ref, o_ref, *, f_chunks):
    mix = jnp.concatenate([r_ref[...], a_ref[...]], axis=1)
    x1 = x_ref[...] + jnp.dot(mix, wo_ref[...], preferred_element_type=F32)
    h = (x1 * lax.rsqrt(jnp.mean(x1 * x1, axis=-1, keepdims=True) + EPS) * g_ref[...]).astype(BF16)
    acc = x1
    for f0, f1 in f_chunks:
        gate = jnp.dot(h, wg_ref[:, f0:f1], preferred_element_type=F32)
        up = jnp.dot(h, wu_ref[:, f0:f1], preferred_element_type=F32)
        act = (gate * jax.nn.sigmoid(gate) * up).astype(BF16)
        acc = acc + jnp.dot(act, wd_ref[f0:f1, :], preferred_element_type=F32)
    o_ref[...] = acc


def _f_chunks(d_ff, mxu_cols=256, max_cols=1536):
    out, f0 = [], 0
    while f0 < d_ff:
        f1 = min(d_ff, f0 + max_cols)
        out.append((f0, f1))
        f0 = f1
    assert all((a % mxu_cols == 0) for a, _ in out)
    return tuple(out)


def _post(x2, ret2, da2, wo, g, wg, wu, wd, tm):
    m, d = x2.shape
    rw, aw = ret2.shape[1], da2.shape[1]
    d_ff = wg.shape[1]
    const = lambda shape: pl.BlockSpec(shape, lambda i: (0, 0), pipeline_mode=pl.Buffered(1))
    return pl.pallas_call(
        functools.partial(_post_kernel, f_chunks=_f_chunks(d_ff)),
        grid=(m // tm,),
        in_specs=[
            pl.BlockSpec((tm, d), lambda i: (i, 0)),
            pl.BlockSpec((tm, rw), lambda i: (i, 0)),
            pl.BlockSpec((tm, aw), lambda i: (i, 0)),
            const((rw + aw, d)),
            const((1, d)),
            const((d, d_ff)),
            const((d, d_ff)),
            const((d_ff, d)),
        ],
        out_specs=pl.BlockSpec((tm, d), lambda i: (i, 0)),
        out_shape=jax.ShapeDtypeStruct((m, d), F32),
        compiler_params=_cparams(("parallel",)),
        name="post",
    )(x2, ret2, da2, wo, g, wg, wu, wd)


def _slope_pieces(n_heads):
    slopes = (2.0 ** (-8.0 * np.arange(1, n_heads + 1, dtype=np.float64) / n_heads) * LOG2E).astype(np.float32)
    cols = np.zeros((n_heads, 2, DIFF_HEAD_DIM, 1), np.float32)
    rem = slopes.astype(np.float64)
    for p in range(3):
        piece = rem.astype(BF16).astype(np.float64)
        for dgt in range(2):
            cols[:, 0, 2 * p + dgt, 0] = piece
            cols[:, 1, N_AUG // 2 + 2 * p + dgt, 0] = piece
        rem = rem - piece
    return tuple(float(v) for v in slopes), jnp.asarray(cols)


def kernel(x, attn_norm_g, w_in, ret_decay_fwd, ret_decay_bwd, ret_norm_g, dq_norm_g, dk_norm_g,
           lambda_q1, lambda_k1, lambda_q2, lambda_k2, diff_norm_g, w_out, ffn_norm_g,
           w_gate, w_up, w_down):
    b, s, d = x.shape
    depth = w_in.shape[0]
    ret_width = d // 2
    diff_width = d - ret_width
    n_dheads = diff_width // DIFF_V_DIM
    tm = 512
    tq = 512
    chunk = 128
    slopes_l2, cpieces = _slope_pieces(n_dheads)

    x2 = x.reshape(b * s, d)
    for l in range(depth):
        lam_init = 0.8 - 0.6 * math.exp(-0.3 * l)
        proj = _in_proj(x2, attn_norm_g[l][None], w_in[l].astype(BF16), tm)
        proj3 = proj.reshape(b, s, -1)

        dec = jnp.stack([ret_decay_fwd[l], ret_decay_bwd[l]]).astype(F32)
        gn_lanes = jnp.tile(ret_norm_g[l].astype(F32), LANES // RET_HEAD_DIM)[None]
        ret = _retention(dec, proj3, gn_lanes, ret_width, chunk)

        qt, ka, vt = _dprep(proj3, dq_norm_g[l].astype(F32)[:, None], dk_norm_g[l].astype(F32)[:, None],
                            cpieces, n_dheads, tq, 4 * ret_width)
        lam_params = jnp.stack([lambda_q1[l], lambda_k1[l], lambda_q2[l], lambda_k2[l]]).astype(F32)
        da = _diffattn(lam_params, qt, ka, vt, diff_norm_g[l].astype(F32)[:, None], tq, lam_init, slopes_l2)

        x2 = _post(x2, ret.reshape(b * s, ret_width), da.reshape(b * s, diff_width),
                   w_out[l].astype(BF16), ffn_norm_g[l][None],
                   w_gate[l].astype(BF16), w_up[l].astype(BF16), w_down[l].astype(BF16), tm)
    return x2.reshape(b, s, d)
```

```python
import functools
import math

import numpy as np
import jax
import jax.numpy as jnp
from jax import lax
from jax.experimental import pallas as pl
from jax.experimental.pallas import tpu as pltpu

EPS = 1e-6
LOG2E = 1.4426950408889634

RET_HEAD_DIM = 64
DIFF_HEAD_DIM = 64
DIFF_V_DIM = 128
LANES = 128
VMEM_LIMIT = 56 * 1024 * 1024

BF16 = jnp.bfloat16
F32 = jnp.float32


def _cparams(sem):
    return pltpu.CompilerParams(dimension_semantics=sem, vmem_limit_bytes=VMEM_LIMIT)


def _in_proj_kernel(x_ref, g_ref, w_ref, o_ref):
    x = x_ref[...]
    y = x * lax.rsqrt(jnp.mean(x * x, axis=-1, keepdims=True) + EPS) * g_ref[...]
    o_ref[...] = jnp.dot(y.astype(BF16), w_ref[...], preferred_element_type=F32).astype(BF16)


def _in_proj(x2, g, w, tm):
    m, d = x2.shape
    n = w.shape[1]
    return pl.pallas_call(
        _in_proj_kernel,
        grid=(m // tm,),
        in_specs=[
            pl.BlockSpec((tm, d), lambda i: (i, 0)),
            pl.BlockSpec((1, d), lambda i: (0, 0)),
            pl.BlockSpec((d, n), lambda i: (0, 0)),
        ],
        out_specs=pl.BlockSpec((tm, n), lambda i: (i, 0)),
        out_shape=jax.ShapeDtypeStruct((m, n), BF16),
        compiler_params=_cparams(("parallel",)),
        name="in_proj",
    )(x2, g, w)


N_AUG = 12
POS_RADIX = 32


def _pos_digits(pos):
    hi = (pos // POS_RADIX) * POS_RADIX
    lo = pos % POS_RADIX
    return hi.astype(F32), lo.astype(F32)


def _dprep_kernel(q_ref, k_ref, v_ref, gq_ref, gk_ref, cp_ref, qt_ref, ka_ref, vt_ref, *, n_heads):
    ts = q_ref.shape[0]
    ng = 2 * n_heads
    hd = DIFF_HEAD_DIM

    def head_norm_t(ref, g_col, scale):
        xt = ref[...].astype(F32).T.reshape(ng, hd, ts)
        ms = jnp.mean(xt * xt, axis=1, keepdims=True)
        return xt * lax.rsqrt(ms + EPS) * (g_col[None] * scale)

    qn = head_norm_t(q_ref, gq_ref[...], DIFF_HEAD_DIM ** -0.5 * LOG2E)
    kn = head_norm_t(k_ref, gk_ref[...], 1.0)

    pos = lax.broadcasted_iota(jnp.int32, (1, ts), 1)
    hi, lo = _pos_digits(pos)
    r = lax.broadcasted_iota(jnp.int32, (hd, 1), 0)
    digit = jnp.where(r % 2 == 0, hi, lo)
    k_digits = jnp.where(r < N_AUG // 2, digit, 0.0)
    q_digits = jnp.where((r >= N_AUG // 2) & (r < N_AUG), -digit, 0.0)
    for g in range(ng):
        q_aug = jnp.concatenate([qn[g], q_digits + cp_ref[g // 2, 0]], axis=0)
        k_aug = jnp.concatenate([kn[g], k_digits + cp_ref[g // 2, 1]], axis=0)
        qt_ref[g] = q_aug.astype(BF16)
        ka_ref[g] = k_aug.T.astype(BF16)
    vt_ref[...] = v_ref[...].astype(F32).T.reshape(n_heads, DIFF_V_DIM, ts).astype(BF16)


def _dprep(proj3, gq, gk, cpieces, n_heads, ts, col0):
    b, s, _ = proj3.shape
    ng = 2 * n_heads
    assert n_heads * DIFF_V_DIM == 512
    cq, ck, cv = col0 // 512, col0 // 512 + 1, col0 // 512 + 2
    return pl.pallas_call(
        functools.partial(_dprep_kernel, n_heads=n_heads),
        grid=(b, s // ts),
        in_specs=[
            pl.BlockSpec((None, ts, 512), lambda i, j: (i, j, cq)),
            pl.BlockSpec((None, ts, 512), lambda i, j: (i, j, ck)),
            pl.BlockSpec((None, ts, 512), lambda i, j: (i, j, cv)),
            pl.BlockSpec((DIFF_HEAD_DIM, 1), lambda i, j: (0, 0)),
            pl.BlockSpec((DIFF_HEAD_DIM, 1), lambda i, j: (0, 0)),
            pl.BlockSpec((n_heads, 2, DIFF_HEAD_DIM, 1), lambda i, j: (0, 0, 0, 0)),
        ],
        out_specs=[
            pl.BlockSpec((None, ng, LANES, ts), lambda i, j: (i, 0, 0, j)),
            pl.BlockSpec((None, ng, ts, LANES), lambda i, j: (i, 0, j, 0)),
            pl.BlockSpec((None, n_heads, DIFF_V_DIM, ts), lambda i, j: (i, 0, 0, j)),
        ],
        out_shape=[
            jax.ShapeDtypeStruct((b, ng, LANES, s), BF16),
            jax.ShapeDtypeStruct((b, ng, s, LANES), BF16),
            jax.ShapeDtypeStruct((b, n_heads, DIFF_V_DIM, s), BF16),
        ],
        compiler_params=_cparams(("parallel", "parallel")),
        name="dprep",
    )(proj3, proj3, proj3, gq, gk, cpieces)


NEG_BIG = -1e30


def _diffattn_kernel(lam_ref, qt_ref, ka_ref, vt_ref, gd_ref, o_ref,
                     qs_ref, m_ref, l_ref, acc_ref, s_refs, p_refs, bmax_refs,
                     *, tq, lam_init, slopes_l2):
    h = pl.program_id(1)
    qi = pl.program_id(2)
    s_len = ka_ref.shape[1]
    nk = s_len // tq
    assert nk % 2 == 0
    slope = jnp.float32(0.0)
    for hh, sv in enumerate(slopes_l2):
        slope = jnp.where(h == hh, jnp.float32(sv), slope)

    row = lax.broadcasted_iota(jnp.int32, (LANES, 1), 0)
    sign = jnp.where(row < DIFF_HEAD_DIM, 1.0, -1.0).astype(BF16)
    for mp in range(2):
        q = qt_ref[mp]
        qs_ref[0, mp] = q
        qs_ref[1, mp] = q * sign

    m_ref[...] = jnp.full(m_ref.shape, NEG_BIG, F32)
    l_ref[...] = jnp.zeros(l_ref.shape, F32)
    acc_ref[...] = jnp.zeros(acc_ref.shape, F32)
    p_refs[1][...] = jnp.zeros(p_refs[1].shape, BF16)

    def scores(j, slot):
        start = pl.multiple_of(j * tq, tq)
        above = (j > qi).astype(jnp.int32)
        for mp in range(2):
            k = ka_ref[mp, pl.ds(start, tq), :]
            s = jnp.dot(k, qs_ref[above, mp], preferred_element_type=F32)
            s_refs[slot][mp] = s
            bmax_refs[slot][mp] = jnp.max(s, axis=0, keepdims=True)

    def fix_diagonal(j, slot):
        @pl.when(j == qi)
        def _():
            ii = lax.broadcasted_iota(jnp.int32, (tq, tq), 0)
            jj = lax.broadcasted_iota(jnp.int32, (tq, tq), 1)
            corr = (2.0 * slope) * jnp.maximum(ii - jj, 0).astype(F32)
            for mp in range(2):
                s = s_refs[slot][mp] - corr
                s_refs[slot][mp] = s
                bmax_refs[slot][mp] = jnp.max(s, axis=0, keepdims=True)

    def softmax(j, slot):
        off = slope * (jnp.abs(qi - j) * tq).astype(F32)
        alphas = []
        for mp in range(2):
            m_old = m_ref[mp]
            m_new = jnp.maximum(m_old, bmax_refs[slot][mp] - off)
            p = jnp.exp2(s_refs[slot][mp] - (m_new + off))
            alpha = jnp.exp2(m_old - m_new)
            l_ref[mp] = alpha * l_ref[mp] + jnp.sum(p, axis=0, keepdims=True)
            m_ref[mp] = m_new
            p_refs[slot][:, mp * tq:(mp + 1) * tq] = p.astype(BF16)
            alphas.append(alpha)
        return alphas

    def weighted_values(j, slot):
        start = pl.multiple_of(j * tq, tq)
        return jnp.dot(vt_ref[:, pl.ds(start, tq)], p_refs[slot][...], preferred_element_type=F32)

    def region(j, slot):
        pv = weighted_values(jnp.maximum(j - 1, 0), 1 - slot)
        alphas = softmax(j, slot)
        scores(jnp.minimum(j + 1, nk - 1), 1 - slot)
        for mp in range(2):
            acc_ref[mp] = (acc_ref[mp] + pv[:, mp * tq:(mp + 1) * tq]) * alphas[mp]

    scores(0, 0)

    def pair(i, c):
        fix_diagonal(2 * i, 0)
        region(2 * i, 0)
        fix_diagonal(2 * i + 1, 1)
        region(2 * i + 1, 1)
        return c

    lax.fori_loop(0, nk // 2, pair, 0)
    pv = weighted_values(nk - 1, 1)
    for mp in range(2):
        acc_ref[mp] += pv[:, mp * tq:(mp + 1) * tq]

    la = lam_ref[...]
    lam = (jnp.exp(jnp.sum(la[0:1] * la[1:2], axis=-1, keepdims=True))
           - jnp.exp(jnp.sum(la[2:3] * la[3:4], axis=-1, keepdims=True)) + lam_init)
    o1 = acc_ref[0] / l_ref[0]
    o2 = acc_ref[1] / l_ref[1]
    da = o1 - lam * o2
    y = da * lax.rsqrt(jnp.mean(da * da, axis=0, keepdims=True) + EPS)
    y = y * (gd_ref[...] * (1.0 - lam_init))
    o_ref[...] = y.T.astype(BF16)


def _diffattn(lam_params, qt, ka, vt, gd_col, tq, lam_init, slopes_l2):
    b, ng, _, s = qt.shape
    n_heads = ng // 2
    kern = functools.partial(_diffattn_kernel, tq=tq, lam_init=lam_init, slopes_l2=slopes_l2)
    return pl.pallas_call(
        kern,
        grid=(b, n_heads, s // tq),
        in_specs=[
            pl.BlockSpec((4, DIFF_HEAD_DIM), lambda i, h, q: (0, 0)),
            pl.BlockSpec((None, 2, LANES, tq), lambda i, h, q: (i, h, 0, q)),
            pl.BlockSpec((None, 2, s, LANES), lambda i, h, q: (i, h, 0, 0)),
            pl.BlockSpec((None, None, DIFF_V_DIM, s), lambda i, h, q: (i, h, 0, 0)),
            pl.BlockSpec((DIFF_V_DIM, 1), lambda i, h, q: (0, 0)),
        ],
        out_specs=pl.BlockSpec((None, tq, DIFF_V_DIM), lambda i, h, q: (i, q, h)),
        out_shape=jax.ShapeDtypeStruct((b, s, n_heads * DIFF_V_DIM), BF16),
        scratch_shapes=[
            pltpu.VMEM((2, 2, LANES, tq), BF16),
            pltpu.VMEM((2, 1, tq), F32),
            pltpu.VMEM((2, 1, tq), F32),
            pltpu.VMEM((2, DIFF_V_DIM, tq), F32),
            [pltpu.VMEM((2, tq, tq), F32)] * 2,
            [pltpu.VMEM((tq, 2 * tq), BF16)] * 2,
            [pltpu.VMEM((2, 1, tq), F32)] * 2,
        ],
        compiler_params=_cparams(("parallel", "parallel", "parallel")),
        name="diffattn",
    )(lam_params, qt, ka, vt, gd_col)


def _retention_kernel(dec_ref, q_ref, k_ref, v_ref, g_ref, gn_ref, o_ref, rf_ref, *, chunk):
    j = pl.program_id(1)
    s_len = q_ref.shape[0]
    n_chunks = s_len // chunk
    hd = RET_HEAD_DIM
    c = chunk

    lane = lax.broadcasted_iota(jnp.int32, (1, LANES), 1)
    rowi = lax.broadcasted_iota(jnp.int32, (LANES, 1), 0)
    first_l = lane < hd
    first_r = rowi < hd

    def per_lane(d):
        return -jnp.exp(jnp.where(first_l, dec_ref[d, 2 * j], dec_ref[d, 2 * j + 1]))

    def per_row(d):
        return -jnp.exp(jnp.where(first_r, dec_ref[d, 2 * j], dec_ref[d, 2 * j + 1]))

    lgf_l, lgb_l = per_lane(0), per_lane(1)
    lgf_r, lgb_r = per_row(0), per_row(1)

    pos_r = lax.broadcasted_iota(jnp.int32, (c, 1), 0).astype(F32)
    wk_f = jnp.exp(lgf_l * (c - 1.0 - pos_r))
    wq_f = jnp.exp(lgf_l * (pos_r + 1.0))
    wk_b = jnp.exp(lgb_l * pos_r)
    wq_b = jnp.exp(lgb_l * (c - pos_r))
    dc_f = jnp.exp(lgf_r * float(c))
    dc_b = jnp.exp(lgb_r * float(c))

    tt = lax.broadcasted_iota(jnp.int32, (c, c), 0)
    ss = lax.broadcasted_iota(jnp.int32, (c, c), 1)
    dist = (tt - ss).astype(F32)

    def dmask(hh):
        lf = -jnp.exp(jnp.full((1, 1), dec_ref[0, 2 * j + hh], F32))
        lb = -jnp.exp(jnp.full((1, 1), dec_ref[1, 2 * j + hh], F32))
        return jnp.where(dist >= 0, jnp.exp(lf * jnp.maximum(dist, 0.0)),
                         jnp.exp(lb * jnp.maximum(-dist, 0.0)))

    d0, d1 = dmask(0), dmask(1)
    bd = (first_r == first_l).astype(F32)
    m0 = first_l.astype(F32)
    m1 = 1.0 - m0
    kscale = RET_HEAD_DIM ** -0.5

    def fwd(n, rf):
        r0 = pl.multiple_of(n * c, c)
        rf_ref[n] = rf.astype(BF16)
        kc = k_ref[pl.ds(r0, c), :].astype(F32) * kscale
        vc = v_ref[pl.ds(r0, c), :]
        kv = lax.dot_general((kc * wk_f).astype(BF16), vc, (((0,), (0,)), ((), ())),
                             preferred_element_type=F32)
        return dc_f * rf + kv * bd

    lax.fori_loop(0, n_chunks, fwd, jnp.zeros((LANES, LANES), F32))

    gn = gn_ref[...]

    def bwd(it, rb):
        n = n_chunks - 1 - it
        r0 = pl.multiple_of(n * c, c)
        qc = q_ref[pl.ds(r0, c), :].astype(F32)
        kc = k_ref[pl.ds(r0, c), :].astype(F32) * kscale
        vc = v_ref[pl.ds(r0, c), :].astype(F32)
        kcb = kc.astype(BF16)
        nt = (((1,), (1,)), ((), ()))
        s0 = lax.dot_general((qc * m0).astype(BF16), kcb, nt, preferred_element_type=F32) * d0
        s1 = lax.dot_general((qc * m1).astype(BF16), kcb, nt, preferred_element_type=F32) * d1
        a = jnp.concatenate([s0, s1, qc * wq_f, qc * wq_b], axis=1).astype(BF16)
        bm = jnp.concatenate([(vc * m0).astype(BF16), (vc * m1).astype(BF16),
                              rf_ref[n], rb.astype(BF16)], axis=0)
        o = jnp.dot(a, bm, preferred_element_type=F32)
        o2 = o * o
        ms0 = jnp.sum(o2 * m0, axis=-1, keepdims=True)
        ms1 = jnp.sum(o2 * m1, axis=-1, keepdims=True)
        ms = jnp.where(first_l, ms0, ms1) * (1.0 / hd)
        y = o * lax.rsqrt(ms + EPS) * gn
        gate = g_ref[pl.ds(r0, c), :].astype(F32)
        y = y * (gate * jax.nn.sigmoid(gate))
        o_ref[pl.ds(r0, c), :] = y.astype(BF16)
        kv = lax.dot_general((kc * wk_b).astype(BF16), vc.astype(BF16), (((0,), (0,)), ((), ())),
                             preferred_element_type=F32)
        return dc_b * rb + kv * bd

    lax.fori_loop(0, n_chunks, bwd, jnp.zeros((LANES, LANES), F32))


def _retention(dec, proj3, gn_lanes, ret_width, chunk):
    b, s, _ = proj3.shape
    npair = ret_width // LANES
    assert chunk == LANES
    blk = lambda o: pl.BlockSpec((None, s, LANES), lambda i, j, o=o: (i, 0, o * npair + j))
    return pl.pallas_call(
        functools.partial(_retention_kernel, chunk=chunk),
        grid=(b, npair),
        in_specs=[
            pl.BlockSpec(memory_space=pltpu.SMEM),
            blk(0), blk(1), blk(2), blk(3),
            pl.BlockSpec((1, LANES), lambda i, j: (0, 0)),
        ],
        out_specs=pl.BlockSpec((None, s, LANES), lambda i, j: (i, 0, j)),
        out_shape=jax.ShapeDtypeStruct((b, s, ret_width), BF16),
        scratch_shapes=[pltpu.VMEM((s // chunk, LANES, LANES), BF16)],
        compiler_params=_cparams(("parallel", "parallel")),
        name="retention",
    )(dec, proj3, proj3, proj3, proj3, gn_lanes)


def _post_kernel(x_ref, r_ref, a_ref, wo_ref, g_ref, wg_ref, wu_ref, wd_ref, o_ref, *, f_chunks):
    mix = jnp.concatenate([r_ref[...], a_ref[...]], axis=1)
    x1 = x_ref[...] + jnp.dot(mix, wo_ref[...], preferred_element_type=F32)
    h = (x1 * lax.rsqrt(jnp.mean(x1 * x1, axis=-1, keepdims=True) + EPS) * g_ref[...]).astype(BF16)
    acc = x1
    for f0, f1 in f_chunks:
        gate = jnp.dot(h, wg_ref[:, f0:f1], preferred_element_type=F32)
        up = jnp.dot(h, wu_ref[:, f0:f1], preferred_element_type=F32)
        act = (gate * jax.nn.sigmoid(gate) * up).astype(BF16)
        acc = acc + jnp.dot(act, wd_ref[f0:f1, :], preferred_element_type=F32)
    o_ref[...] = acc


def _f_chunks(d_ff, mxu_cols=256, max_cols=1536):
    out, f0 = [], 0
    while f0 < d_ff:
        f1 = min(d_ff, f0 + max_cols)
        out.append((f0, f1))
        f0 = f1
    assert all((a % mxu_cols == 0) for a, _ in out)
    return tuple(out)


def _post(x2, ret2, da2, wo, g, wg, wu, wd, tm):
    m, d = x2.shape
    rw, aw = ret2.shape[1], da2.shape[1]
    d_ff = wg.shape[1]
    const = lambda shape: pl.BlockSpec(shape, lambda i: (0, 0), pipeline_mode=pl.Buffered(1))
    return pl.pallas_call(
        functools.partial(_post_kernel, f_chunks=_f_chunks(d_ff)),
        grid=(m // tm,),
        in_specs=[
            pl.BlockSpec((tm, d), lambda i: (i, 0)),
            pl.BlockSpec((tm, rw), lambda i: (i, 0)),
            pl.BlockSpec((tm, aw), lambda i: (i, 0)),
            const((rw + aw, d)),
            const((1, d)),
            const((d, d_ff)),
            const((d, d_ff)),
            const((d_ff, d)),
        ],
        out_specs=pl.BlockSpec((tm, d), lambda i: (i, 0)),
        out_shape=jax.ShapeDtypeStruct((m, d), F32),
        compiler_params=_cparams(("parallel",)),
        name="post",
    )(x2, ret2, da2, wo, g, wg, wu, wd)


def _slope_pieces(n_heads):
    slopes = (2.0 ** (-8.0 * np.arange(1, n_heads + 1, dtype=np.float64) / n_heads) * LOG2E).astype(np.float32)
    cols = np.zeros((n_heads, 2, DIFF_HEAD_DIM, 1), np.float32)
    rem = slopes.astype(np.float64)
    for p in range(3):
        piece = rem.astype(BF16).astype(np.float64)
        for dgt in range(2):
            cols[:, 0, 2 * p + dgt, 0] = piece
            cols[:, 1, N_AUG // 2 + 2 * p + dgt, 0] = piece
        rem = rem - piece
    return tuple(float(v) for v in slopes), jnp.asarray(cols)


def kernel(x, attn_norm_g, w_in, ret_decay_fwd, ret_decay_bwd, ret_norm_g, dq_norm_g, dk_norm_g,
           lambda_q1, lambda_k1, lambda_q2, lambda_k2, diff_norm_g, w_out, ffn_norm_g,
           w_gate, w_up, w_down):
    b, s, d = x.shape
    depth = w_in.shape[0]
    ret_width = d // 2
    diff_width = d - ret_width
    n_dheads = diff_width // DIFF_V_DIM
    tm = 512
    tq = 512
    chunk = 128
    slopes_l2, cpieces = _slope_pieces(n_dheads)

    x2 = x.reshape(b * s, d)
    for l in range(depth):
        lam_init = 0.8 - 0.6 * math.exp(-0.3 * l)
        proj = _in_proj(x2, attn_norm_g[l][None], w_in[l].astype(BF16), tm)
        proj3 = proj.reshape(b, s, -1)

        dec = jnp.stack([ret_decay_fwd[l], ret_decay_bwd[l]]).astype(F32)
        gn_lanes = jnp.tile(ret_norm_g[l].astype(F32), LANES // RET_HEAD_DIM)[None]
        ret = _retention(dec, proj3, gn_lanes, ret_width, chunk)

        qt, ka, vt = _dprep(proj3, dq_norm_g[l].astype(F32)[:, None], dk_norm_g[l].astype(F32)[:, None],
                            cpieces, n_dheads, tq, 4 * ret_width)
        lam_params = jnp.stack([lambda_q1[l], lambda_k1[l], lambda_q2[l], lambda_k2[l]]).astype(F32)
        da = _diffattn(lam_params, qt, ka, vt, diff_norm_g[l].astype(F32)[:, None], tq, lam_init, slopes_l2)

        x2 = _post(x2, ret.reshape(b * s, ret_width), da.reshape(b * s, diff_width),
                   w_out[l].astype(BF16), ffn_norm_g[l][None],
                   w_gate[l].astype(BF16), w_up[l].astype(BF16), w_down[l].astype(BF16), tm)
    return x2.reshape(b, s, d)
```

```python
import functools
import math

import numpy as np
import jax
import jax.numpy as jnp
from jax import lax
from jax.experimental import pallas as pl
from jax.experimental.pallas import tpu as pltpu

EPS = 1e-6
LOG2E = 1.4426950408889634

RET_HEAD_DIM = 64
DIFF_HEAD_DIM = 64
DIFF_V_DIM = 128
LANES = 128
VMEM_LIMIT = 56 * 1024 * 1024

BF16 = jnp.bfloat16
F32 = jnp.float32


def _cparams(sem):
    return pltpu.CompilerParams(dimension_semantics=sem, vmem_limit_bytes=VMEM_LIMIT)


def _in_proj_kernel(x_ref, g_ref, w_ref, o_ref):
    x = x_ref[...]
    y = x * lax.rsqrt(jnp.mean(x * x, axis=-1, keepdims=True) + EPS) * g_ref[...]
    o_ref[...] = jnp.dot(y.astype(BF16), w_ref[...], preferred_element_type=F32).astype(BF16)


def _in_proj(x2, g, w, tm):
    m, d = x2.shape
    n = w.shape[1]
    return pl.pallas_call(
        _in_proj_kernel,
        grid=(m // tm,),
        in_specs=[
            pl.BlockSpec((tm, d), lambda i: (i, 0)),
            pl.BlockSpec((1, d), lambda i: (0, 0)),
            pl.BlockSpec((d, n), lambda i: (0, 0)),
        ],
        out_specs=pl.BlockSpec((tm, n), lambda i: (i, 0)),
        out_shape=jax.ShapeDtypeStruct((m, n), BF16),
        compiler_params=_cparams(("parallel",)),
        name="in_proj",
    )(x2, g, w)


N_DIGITS = 3
N_PIECES = 3
N_AUG = 2 * N_DIGITS * N_PIECES
POS_RADIX = 32


def _pos_digits(pos):
    d0 = pos % POS_RADIX
    d1 = (pos // POS_RADIX) % POS_RADIX * POS_RADIX
    d2 = pos // (POS_RADIX * POS_RADIX) * (POS_RADIX * POS_RADIX)
    return [d.astype(F32) for d in (d0, d1, d2)]


def _dprep_kernel(q_ref, k_ref, v_ref, gq_ref, gk_ref, cp_ref,
                  qt_ref, ka_ref, vt_ref, qn2_ref, kn2_ref, *, n_heads):
    ts = q_ref.shape[0]
    ng = 2 * n_heads
    hd = DIFF_HEAD_DIM
    half = N_AUG // 2

    def head_norm_t(ref, g_col, scale):
        xt = ref[...].astype(F32).T.reshape(ng, hd, ts)
        ms = jnp.mean(xt * xt, axis=1, keepdims=True)
        return (xt * lax.rsqrt(ms + EPS) * (g_col[None] * scale)).astype(BF16)

    qn = head_norm_t(q_ref, gq_ref[...], DIFF_HEAD_DIM ** -0.5 * LOG2E)
    kn = head_norm_t(k_ref, gk_ref[...], 1.0)
    qn2_ref[...] = jnp.sum(jnp.square(qn.astype(F32)), axis=1, keepdims=True)
    kn2_ref[...] = jnp.sum(jnp.square(kn.astype(F32)), axis=1, keepdims=True)

    pos = pl.program_id(1) * ts + lax.broadcasted_iota(jnp.int32, (1, ts), 1)
    d0, d1, d2 = _pos_digits(pos)
    r = lax.broadcasted_iota(jnp.int32, (hd, 1), 0)
    which = jnp.where(r < half, r, r - half) % N_DIGITS
    digit = jnp.where(which == 0, d0, jnp.where(which == 1, d1, d2))
    k_digits = jnp.where(r < half, digit, 0.0)
    q_digits = jnp.where((r >= half) & (r < N_AUG), -digit, 0.0)
    for g in range(ng):
        q_rows = (q_digits + cp_ref[g // 2, 0]).astype(BF16)
        k_rows = (k_digits + cp_ref[g // 2, 1]).astype(BF16)
        qt_ref[g] = jnp.concatenate([qn[g], q_rows], axis=0)
        ka_ref[g] = jnp.concatenate([kn[g], k_rows], axis=0).astype(F32).T.astype(BF16)
    vt_ref[...] = v_ref[...].astype(F32).T.reshape(n_heads, DIFF_V_DIM, ts).astype(BF16)


def _dprep(proj3, gq, gk, cpieces, n_heads, ts, col0):
    b, s, _ = proj3.shape
    ng = 2 * n_heads
    assert n_heads * DIFF_V_DIM == 512
    cq, ck, cv = col0 // 512, col0 // 512 + 1, col0 // 512 + 2
    return pl.pallas_call(
        functools.partial(_dprep_kernel, n_heads=n_heads),
        grid=(b, s // ts),
        in_specs=[
            pl.BlockSpec((None, ts, 512), lambda i, j: (i, j, cq)),
            pl.BlockSpec((None, ts, 512), lambda i, j: (i, j, ck)),
            pl.BlockSpec((None, ts, 512), lambda i, j: (i, j, cv)),
            pl.BlockSpec((DIFF_HEAD_DIM, 1), lambda i, j: (0, 0)),
            pl.BlockSpec((DIFF_HEAD_DIM, 1), lambda i, j: (0, 0)),
            pl.BlockSpec((n_heads, 2, DIFF_HEAD_DIM, 1), lambda i, j: (0, 0, 0, 0)),
        ],
        out_specs=[
            pl.BlockSpec((None, ng, LANES, ts), lambda i, j: (i, 0, 0, j)),
            pl.BlockSpec((None, ng, ts, LANES), lambda i, j: (i, 0, j, 0)),
            pl.BlockSpec((None, n_heads, DIFF_V_DIM, ts), lambda i, j: (i, 0, 0, j)),
            pl.BlockSpec((None, ng, 1, ts), lambda i, j: (i, 0, 0, j)),
            pl.BlockSpec((None, ng, 1, ts), lambda i, j: (i, 0, 0, j)),
        ],
        out_shape=[
            jax.ShapeDtypeStruct((b, ng, LANES, s), BF16),
            jax.ShapeDtypeStruct((b, ng, s, LANES), BF16),
            jax.ShapeDtypeStruct((b, n_heads, DIFF_V_DIM, s), BF16),
            jax.ShapeDtypeStruct((b, ng, 1, s), F32),
            jax.ShapeDtypeStruct((b, ng, 1, s), F32),
        ],
        compiler_params=_cparams(("parallel", "parallel")),
        name="dprep",
    )(proj3, proj3, proj3, gq, gk, cpieces)


NEG_BIG = -1e30
SAFE_LOG2_RANGE = 96.0


def _diffattn_kernel(lam_ref, qt_ref, ka_ref, vt_ref, qn2_ref, kn2_ref, gd_ref, o_ref,
                     qs_ref, corr_ref, m_ref, l_ref, acc_ref, s_refs, p_refs, bmax_refs,
                     *, tq, lam_init, slopes_l2):
    h = pl.program_id(1)
    qi = pl.program_id(2)
    s_len = ka_ref.shape[1]
    nk = s_len // tq
    assert nk % 2 == 0
    slope = jnp.float32(0.0)
    for hh, sv in enumerate(slopes_l2):
        slope = jnp.where(h == hh, jnp.float32(sv), slope)

    @pl.when(qi == 0)
    def _():
        ii = lax.broadcasted_iota(jnp.int32, (tq, tq), 0)
        jj = lax.broadcasted_iota(jnp.int32, (tq, tq), 1)
        corr_ref[0] = jnp.zeros((tq, tq), F32)
        corr_ref[1] = (2.0 * slope) * jnp.maximum(ii - jj, 0).astype(F32)

    row = lax.broadcasted_iota(jnp.int32, (LANES, 1), 0)
    sign = jnp.where(row < DIFF_HEAD_DIM, 1.0, -1.0).astype(BF16)
    for mp in range(2):
        q = qt_ref[mp]
        qs_ref[0, mp] = q
        qs_ref[1, mp] = q * sign

    l_ref[...] = jnp.zeros(l_ref.shape, F32)
    acc_ref[...] = jnp.zeros(acc_ref.shape, F32)
    p_refs[1][...] = jnp.zeros(p_refs[1].shape, BF16)

    def raw_scores(j, mp):
        start = pl.multiple_of(j * tq, tq)
        above = (j > qi).astype(jnp.int32)
        diag = (j == qi).astype(jnp.int32)
        k = ka_ref[mp, pl.ds(start, tq), :]
        return jnp.dot(k, qs_ref[above, mp], preferred_element_type=F32) - corr_ref[diag]

    def weighted_values(j, slot):
        start = pl.multiple_of(j * tq, tq)
        return jnp.dot(vt_ref[:, pl.ds(start, tq)], p_refs[slot][...], preferred_element_type=F32)

    def run_pipeline(region):
        def pair(i, c):
            region(2 * i, 0)
            region(2 * i + 1, 1)
            return c

        lax.fori_loop(0, nk // 2, pair, 0)
        pv = weighted_values(nk - 1, 1)
        for mp in range(2):
            acc_ref[mp] += pv[:, mp * tq:(mp + 1) * tq]

    def bounded_path():
        def region(j, slot):
            pv = weighted_values(jnp.maximum(j - 1, 0), 1 - slot)
            for mp in range(2):
                p = jnp.exp2(raw_scores(j, mp))
                l_ref[mp] += jnp.sum(p, axis=0, keepdims=True)
                p_refs[slot][:, mp * tq:(mp + 1) * tq] = p.astype(BF16)
            for mp in range(2):
                acc_ref[mp] += pv[:, mp * tq:(mp + 1) * tq]

        run_pipeline(region)

    def online_max_path():
        m_ref[...] = jnp.full(m_ref.shape, NEG_BIG, F32)

        def scores(j, slot):
            for mp in range(2):
                s = raw_scores(j, mp)
                s_refs[slot][mp] = s
                bmax_refs[slot][mp] = jnp.max(s, axis=0, keepdims=True)

        def region(j, slot):
            pv = weighted_values(jnp.maximum(j - 1, 0), 1 - slot)
            alphas = []
            for mp in range(2):
                m_old = m_ref[mp]
                m_new = jnp.maximum(m_old, bmax_refs[slot][mp])
                p = jnp.exp2(s_refs[slot][mp] - m_new)
                alpha = jnp.exp2(m_old - m_new)
                l_ref[mp] = alpha * l_ref[mp] + jnp.sum(p, axis=0, keepdims=True)
                m_ref[mp] = m_new
                p_refs[slot][:, mp * tq:(mp + 1) * tq] = p.astype(BF16)
                alphas.append(alpha)
            scores(jnp.minimum(j + 1, nk - 1), 1 - slot)
            for mp in range(2):
                acc_ref[mp] = (acc_ref[mp] + pv[:, mp * tq:(mp + 1) * tq]) * alphas[mp]

        scores(0, 0)
        run_pipeline(region)

    bound2 = jnp.float32(0.0)
    for mp in range(2):
        bound2 = jnp.maximum(bound2, jnp.max(qn2_ref[mp]) * jnp.max(kn2_ref[mp]))
    lax.cond(bound2 <= SAFE_LOG2_RANGE ** 2, bounded_path, online_max_path)

    la = lam_ref[...]
    lam = (jnp.exp(jnp.sum(la[0:1] * la[1:2], axis=-1, keepdims=True))
           - jnp.exp(jnp.sum(la[2:3] * la[3:4], axis=-1, keepdims=True)) + lam_init)
    o1 = acc_ref[0] / l_ref[0]
    o2 = acc_ref[1] / l_ref[1]
    da = o1 - lam * o2
    y = da * lax.rsqrt(jnp.mean(da * da, axis=0, keepdims=True) + EPS)
    y = y * (gd_ref[...] * (1.0 - lam_init))
    o_ref[...] = y.T.astype(BF16)


def _diffattn(lam_params, qt, ka, vt, qn2, kn2, gd_col, tq, lam_init, slopes_l2):
    b, ng, _, s = qt.shape
    n_heads = ng // 2
    kern = functools.partial(_diffattn_kernel, tq=tq, lam_init=lam_init, slopes_l2=slopes_l2)
    return pl.pallas_call(
        kern,
        grid=(b, n_heads, s // tq),
        in_specs=[
            pl.BlockSpec((4, DIFF_HEAD_DIM), lambda i, h, q: (0, 0)),
            pl.BlockSpec((None, 2, LANES, tq), lambda i, h, q: (i, h, 0, q)),
            pl.BlockSpec((None, 2, s, LANES), lambda i, h, q: (i, h, 0, 0)),
            pl.BlockSpec((None, None, DIFF_V_DIM, s), lambda i, h, q: (i, h, 0, 0)),
            pl.BlockSpec((None, 2, 1, tq), lambda i, h, q: (i, h, 0, q)),
            pl.BlockSpec((None, 2, 1, s), lambda i, h, q: (i, h, 0, 0)),
            pl.BlockSpec((DIFF_V_DIM, 1), lambda i, h, q: (0, 0)),
        ],
        out_specs=pl.BlockSpec((None, tq, DIFF_V_DIM), lambda i, h, q: (i, q, h)),
        out_shape=jax.ShapeDtypeStruct((b, s, n_heads * DIFF_V_DIM), BF16),
        scratch_shapes=[
            pltpu.VMEM((2, 2, LANES, tq), BF16),
            pltpu.VMEM((2, tq, tq), F32),
            pltpu.VMEM((2, 1, tq), F32),
            pltpu.VMEM((2, 1, tq), F32),
            pltpu.VMEM((2, DIFF_V_DIM, tq), F32),
            [pltpu.VMEM((2, tq, tq), F32)] * 2,
            [pltpu.VMEM((tq, 2 * tq), BF16)] * 2,
            [pltpu.VMEM((2, 1, tq), F32)] * 2,
        ],
        compiler_params=_cparams(("parallel", "parallel", "arbitrary")),
        name="diffattn",
    )(lam_params, qt, ka, vt, qn2, kn2, gd_col)


def _retention_kernel(dec_ref, q_ref, k_ref, v_ref, g_ref, gn_ref, o_ref,
                      kvf_ref, kvb_ref, rf_ref, rb_ref, a_ref, *, chunk, group):
    j = pl.program_id(1)
    s_len = q_ref.shape[0]
    n_chunks = s_len // chunk
    hd = RET_HEAD_DIM
    c = chunk

    lane = lax.broadcasted_iota(jnp.int32, (1, LANES), 1)
    rowi = lax.broadcasted_iota(jnp.int32, (LANES, 1), 0)
    first_l = lane < hd
    first_r = rowi < hd

    def per_lane(d):
        return -jnp.exp(jnp.where(first_l, dec_ref[d, 2 * j], dec_ref[d, 2 * j + 1]))

    def per_row(d):
        return -jnp.exp(jnp.where(first_r, dec_ref[d, 2 * j], dec_ref[d, 2 * j + 1]))

    lgf_l, lgb_l = per_lane(0), per_lane(1)
    lgf_r, lgb_r = per_row(0), per_row(1)

    pos_r = lax.broadcasted_iota(jnp.int32, (c, 1), 0).astype(F32)
    wk_f = jnp.exp(lgf_l * (c - 1.0 - pos_r))
    wq_f = jnp.exp(lgf_l * (pos_r + 1.0))
    wk_b = jnp.exp(lgb_l * pos_r)
    wq_b = jnp.exp(lgb_l * (c - pos_r))
    dc_f = jnp.exp(lgf_r * float(c))
    dc_b = jnp.exp(lgb_r * float(c))

    tt = lax.broadcasted_iota(jnp.int32, (c, c), 0)
    ss = lax.broadcasted_iota(jnp.int32, (c, c), 1)
    dist = (tt - ss).astype(F32)

    def dmask(hh):
        lf = -jnp.exp(jnp.full((1, 1), dec_ref[0, 2 * j + hh], F32))
        lb = -jnp.exp(jnp.full((1, 1), dec_ref[1, 2 * j + hh], F32))
        return jnp.where(dist >= 0, jnp.exp(lf * jnp.maximum(dist, 0.0)),
                         jnp.exp(lb * jnp.maximum(-dist, 0.0)))

    kscale = RET_HEAD_DIM ** -0.5
    dcat = jnp.concatenate([dmask(0), dmask(1)], axis=1) * kscale
    wk_f = wk_f * kscale
    wk_b = wk_b * kscale
    wq_f = wq_f.astype(BF16)
    wq_b = wq_b.astype(BF16)
    bd = (first_r == first_l).astype(F32)
    m0f = first_l.astype(F32)
    m0 = m0f.astype(BF16)
    m1 = (1.0 - m0f).astype(BF16)

    tn = (((0,), (0,)), ((), ()))
    nt = (((1,), (1,)), ((), ()))
    assert n_chunks % group == 0

    def chunk_local(i, carry):
        for u in range(group):
            n = i * group + u
            r0 = pl.multiple_of(n * c, c)
            qb = q_ref[pl.ds(r0, c), :]
            kb = k_ref[pl.ds(r0, c), :]
            vb = v_ref[pl.ds(r0, c), :]
            kf = kb.astype(F32)
            kvf_ref[n] = lax.dot_general((kf * wk_f).astype(BF16), vb, tn, preferred_element_type=F32) * bd
            kvb_ref[n] = lax.dot_general((kf * wk_b).astype(BF16), vb, tn, preferred_element_type=F32) * bd
            kcat = jnp.concatenate([kb * m0, kb * m1], axis=0)
            s = lax.dot_general(qb, kcat, nt, preferred_element_type=F32) * dcat
            a_ref[n] = jnp.concatenate([s.astype(BF16), qb * wq_f, qb * wq_b], axis=1)
        return carry

    lax.fori_loop(0, n_chunks // group, chunk_local, 0)

    def scan(n, carry):
        rf, rb = carry
        nb = n_chunks - 1 - n
        rf_ref[n] = rf.astype(BF16)
        rb_ref[nb] = rb.astype(BF16)
        return dc_f * rf + kvf_ref[n], dc_b * rb + kvb_ref[nb]

    zero = jnp.zeros((LANES, LANES), F32)
    lax.fori_loop(0, n_chunks, scan, (zero, zero))

    gn = gn_ref[...]

    def outputs(i, carry):
        for u in range(group):
            n = i * group + u
            r0 = pl.multiple_of(n * c, c)
            vb = v_ref[pl.ds(r0, c), :]
            bm = jnp.concatenate([vb * m0, vb * m1, rf_ref[n], rb_ref[n]], axis=0)
            o = jnp.dot(a_ref[n], bm, preferred_element_type=F32)
            o2 = o * o
            ms0 = jnp.sum(o2 * m0f, axis=-1, keepdims=True)
            ms1 = jnp.sum(o2 * (1.0 - m0f), axis=-1, keepdims=True)
            ms = jnp.where(first_l, ms0, ms1) * (1.0 / hd)
            y = o * lax.rsqrt(ms + EPS) * gn
            gate = g_ref[pl.ds(r0, c), :].astype(F32)
            y = y * (gate * jax.nn.sigmoid(gate))
            o_ref[pl.ds(r0, c), :] = y.astype(BF16)
        return carry

    lax.fori_loop(0, n_chunks // group, outputs, 0)


def _retention(dec, proj3, gn_lanes, ret_width, chunk, group=8):
    b, s, _ = proj3.shape
    npair = ret_width // LANES
    assert chunk == LANES
    blk = lambda o: pl.BlockSpec((None, s, LANES), lambda i, j, o=o: (i, 0, o * npair + j))
    return pl.pallas_call(
        functools.partial(_retention_kernel, chunk=chunk, group=group),
        grid=(b, npair),
        in_specs=[
            pl.BlockSpec(memory_space=pltpu.SMEM),
            blk(0), blk(1), blk(2), blk(3),
            pl.BlockSpec((1, LANES), lambda i, j: (0, 0)),
        ],
        out_specs=pl.BlockSpec((None, s, LANES), lambda i, j: (i, 0, j)),
        out_shape=jax.ShapeDtypeStruct((b, s, ret_width), BF16),
        scratch_shapes=(
            [pltpu.VMEM((s // chunk, LANES, LANES), F32)] * 2
            + [pltpu.VMEM((s // chunk, LANES, LANES), BF16)] * 2
            + [pltpu.VMEM((s // chunk, chunk, 4 * chunk), BF16)]
        ),
        compiler_params=_cparams(("parallel", "parallel")),
        name="retention",
    )(dec, proj3, proj3, proj3, proj3, gn_lanes)


def _post_kernel(x_ref, r_ref, a_ref, wo_ref, g_ref, wg_ref, wu_ref, wd_ref, o_ref, *, f_chunks):
    mix = jnp.concatenate([r_ref[...], a_ref[...]], axis=1)
    x1 = x_ref[...] + jnp.dot(mix, wo_ref[...], preferred_element_type=F32)
    h = (x1 * lax.rsqrt(jnp.mean(x1 * x1, axis=-1, keepdims=True) + EPS) * g_ref[...]).astype(BF16)
    acc = x1
    for f0, f1 in f_chunks:
        gate = jnp.dot(h, wg_ref[:, f0:f1], preferred_element_type=F32)
        up = jnp.dot(h, wu_ref[:, f0:f1], preferred_element_type=F32)
        act = (gate * jax.nn.sigmoid(gate) * up).astype(BF16)
        acc = acc + jnp.dot(act, wd_ref[f0:f1, :], preferred_element_type=F32)
    o_ref[...] = acc


def _f_chunks(d_ff, mxu_cols=256, max_cols=1536):
    out, f0 = [], 0
    while f0 < d_ff:
        f1 = min(d_ff, f0 + max_cols)
        out.append((f0, f1))
        f0 = f1
    assert all((a % mxu_cols == 0) for a, _ in out)
    return tuple(out)


def _post(x2, ret2, da2, wo, g, wg, wu, wd, tm):
    m, d = x2.shape
    rw, aw = ret2.shape[1], da2.shape[1]
    d_ff = wg.shape[1]
    const = lambda shape: pl.BlockSpec(shape, lambda i: (0, 0), pipeline_mode=pl.Buffered(1))
    return pl.pallas_call(
        functools.partial(_post_kernel, f_chunks=_f_chunks(d_ff)),
        grid=(m // tm,),
        in_specs=[
            pl.BlockSpec((tm, d), lambda i: (i, 0)),
            pl.BlockSpec((tm, rw), lambda i: (i, 0)),
            pl.BlockSpec((tm, aw), lambda i: (i, 0)),
            const((rw + aw, d)),
            const((1, d)),
            const((d, d_ff)),
            const((d, d_ff)),
            const((d_ff, d)),
        ],
        out_specs=pl.BlockSpec((tm, d), lambda i: (i, 0)),
        out_shape=jax.ShapeDtypeStruct((m, d), F32),
        compiler_params=_cparams(("parallel",)),
        name="post",
    )(x2, ret2, da2, wo, g, wg, wu, wd)


def _slope_pieces(n_heads):
    slopes = (2.0 ** (-8.0 * np.arange(1, n_heads + 1, dtype=np.float64) / n_heads) * LOG2E).astype(np.float32)
    cols = np.zeros((n_heads, 2, DIFF_HEAD_DIM, 1), np.float32)
    rem = slopes.astype(np.float64)
    for p in range(N_PIECES):
        piece = rem.astype(BF16).astype(np.float64)
        for dgt in range(N_DIGITS):
            cols[:, 0, N_DIGITS * p + dgt, 0] = piece
            cols[:, 1, N_AUG // 2 + N_DIGITS * p + dgt, 0] = piece
        rem = rem - piece
    return tuple(float(v) for v in slopes), jnp.asarray(cols)


def kernel(x, attn_norm_g, w_in, ret_decay_fwd, ret_decay_bwd, ret_norm_g, dq_norm_g, dk_norm_g,
           lambda_q1, lambda_k1, lambda_q2, lambda_k2, diff_norm_g, w_out, ffn_norm_g,
           w_gate, w_up, w_down):
    b, s, d = x.shape
    depth = w_in.shape[0]
    ret_width = d // 2
    diff_width = d - ret_width
    n_dheads = diff_width // DIFF_V_DIM
    tm = 512
    tq = 512
    chunk = 128
    slopes_l2, cpieces = _slope_pieces(n_dheads)

    x2 = x.reshape(b * s, d)
    for l in range(depth):
        lam_init = 0.8 - 0.6 * math.exp(-0.3 * l)
        proj = _in_proj(x2, attn_norm_g[l][None], w_in[l].astype(BF16), tm)
        proj3 = proj.reshape(b, s, -1)

        dec = jnp.stack([ret_decay_fwd[l], ret_decay_bwd[l]]).astype(F32)
        gn_lanes = jnp.tile(ret_norm_g[l].astype(F32), LANES // RET_HEAD_DIM)[None]
        ret = _retention(dec, proj3, gn_lanes, ret_width, chunk)

        qt, ka, vt, qn2, kn2 = _dprep(proj3, dq_norm_g[l].astype(F32)[:, None],
                                      dk_norm_g[l].astype(F32)[:, None], cpieces, n_dheads, tq, 4 * ret_width)
        lam_params = jnp.stack([lambda_q1[l], lambda_k1[l], lambda_q2[l], lambda_k2[l]]).astype(F32)
        da = _diffattn(lam_params, qt, ka, vt, qn2, kn2, diff_norm_g[l].astype(F32)[:, None],
                       tq, lam_init, slopes_l2)

        x2 = _post(x2, ret.reshape(b * s, ret_width), da.reshape(b * s, diff_width),
                   w_out[l].astype(BF16), ffn_norm_g[l][None],
                   w_gate[l].astype(BF16), w_up[l].astype(BF16), w_down[l].astype(BF16), tm)
    return x2.reshape(b, s, d)
```

```python
import functools
import math

import numpy as np
import jax
import jax.numpy as jnp
from jax import lax
from jax.experimental import pallas as pl
from jax.experimental.pallas import tpu as pltpu

EPS = 1e-6
LOG2E = 1.4426950408889634

RET_HEAD_DIM = 64
DIFF_HEAD_DIM = 64
DIFF_V_DIM = 128
LANES = 128
VMEM_LIMIT = 56 * 1024 * 1024

BF16 = jnp.bfloat16
F32 = jnp.float32


def _cparams(sem):
    return pltpu.CompilerParams(dimension_semantics=sem, vmem_limit_bytes=VMEM_LIMIT)


def _in_proj_kernel(x_ref, g_ref, w_ref, o_ref):
    x = x_ref[...]
    y = x * lax.rsqrt(jnp.mean(x * x, axis=-1, keepdims=True) + EPS) * g_ref[...]
    o_ref[...] = jnp.dot(y.astype(BF16), w_ref[...], preferred_element_type=F32).astype(BF16)


def _in_proj(x2, g, w, tm):
    m, d = x2.shape
    n = w.shape[1]
    return pl.pallas_call(
        _in_proj_kernel,
        grid=(m // tm,),
        in_specs=[
            pl.BlockSpec((tm, d), lambda i: (i, 0)),
            pl.BlockSpec((1, d), lambda i: (0, 0)),
            pl.BlockSpec((d, n), lambda i: (0, 0)),
        ],
        out_specs=pl.BlockSpec((tm, n), lambda i: (i, 0)),
        out_shape=jax.ShapeDtypeStruct((m, n), BF16),
        compiler_params=_cparams(("parallel",)),
        name="in_proj",
    )(x2, g, w)


N_DIGITS = 3
N_PIECES = 3
N_AUG = 2 * N_DIGITS * N_PIECES
POS_RADIX = 32


def _pos_digits(pos):
    d0 = pos % POS_RADIX
    d1 = (pos // POS_RADIX) % POS_RADIX * POS_RADIX
    d2 = pos // (POS_RADIX * POS_RADIX) * (POS_RADIX * POS_RADIX)
    return [d.astype(F32) for d in (d0, d1, d2)]


def _dprep_kernel(q_ref, k_ref, v_ref, gq_ref, gk_ref, cp_ref,
                  qt_ref, ka_ref, vt_ref, qn2_ref, kn2_ref, *, n_heads):
    ts = q_ref.shape[0]
    ng = 2 * n_heads
    hd = DIFF_HEAD_DIM
    half = N_AUG // 2

    def head_norm_t(ref, g_col, scale):
        xt = ref[...].astype(F32).T.reshape(ng, hd, ts)
        ms = jnp.mean(xt * xt, axis=1, keepdims=True)
        return (xt * lax.rsqrt(ms + EPS) * (g_col[None] * scale)).astype(BF16)

    qn = head_norm_t(q_ref, gq_ref[...], DIFF_HEAD_DIM ** -0.5 * LOG2E)
    kn = head_norm_t(k_ref, gk_ref[...], 1.0)
    qn2_ref[...] = jnp.sum(jnp.square(qn.astype(F32)), axis=1, keepdims=True)
    kn2_ref[...] = jnp.sum(jnp.square(kn.astype(F32)), axis=1, keepdims=True)

    pos = pl.program_id(1) * ts + lax.broadcasted_iota(jnp.int32, (1, ts), 1)
    d0, d1, d2 = _pos_digits(pos)
    r = lax.broadcasted_iota(jnp.int32, (hd, 1), 0)
    which = jnp.where(r < half, r, r - half) % N_DIGITS
    digit = jnp.where(which == 0, d0, jnp.where(which == 1, d1, d2))
    k_digits = jnp.where(r < half, digit, 0.0)
    q_digits = jnp.where((r >= half) & (r < N_AUG), -digit, 0.0)
    for g in range(ng):
        q_rows = (q_digits + cp_ref[g // 2, 0]).astype(BF16)
        k_rows = (k_digits + cp_ref[g // 2, 1]).astype(BF16)
        qt_ref[g] = jnp.concatenate([qn[g], q_rows], axis=0)
        ka_ref[g] = jnp.concatenate([kn[g], k_rows], axis=0).astype(F32).T.astype(BF16)
    vt_ref[...] = v_ref[...].astype(F32).T.reshape(n_heads, DIFF_V_DIM, ts).astype(BF16)


def _dprep(proj3, gq, gk, cpieces, n_heads, ts, col0):
    b, s, _ = proj3.shape
    ng = 2 * n_heads
    assert n_heads * DIFF_V_DIM == 512
    cq, ck, cv = col0 // 512, col0 // 512 + 1, col0 // 512 + 2
    return pl.pallas_call(
        functools.partial(_dprep_kernel, n_heads=n_heads),
        grid=(b, s // ts),
        in_specs=[
            pl.BlockSpec((None, ts, 512), lambda i, j: (i, j, cq)),
            pl.BlockSpec((None, ts, 512), lambda i, j: (i, j, ck)),
            pl.BlockSpec((None, ts, 512), lambda i, j: (i, j, cv)),
            pl.BlockSpec((DIFF_HEAD_DIM, 1), lambda i, j: (0, 0)),
            pl.BlockSpec((DIFF_HEAD_DIM, 1), lambda i, j: (0, 0)),
            pl.BlockSpec((n_heads, 2, DIFF_HEAD_DIM, 1), lambda i, j: (0, 0, 0, 0)),
        ],
        out_specs=[
            pl.BlockSpec((None, ng, LANES, ts), lambda i, j: (i, 0, 0, j)),
            pl.BlockSpec((None, ng, ts, LANES), lambda i, j: (i, 0, j, 0)),
            pl.BlockSpec((None, n_heads, DIFF_V_DIM, ts), lambda i, j: (i, 0, 0, j)),
            pl.BlockSpec((None, ng, 1, ts), lambda i, j: (i, 0, 0, j)),
            pl.BlockSpec((None, ng, 1, ts), lambda i, j: (i, 0, 0, j)),
        ],
        out_shape=[
            jax.ShapeDtypeStruct((b, ng, LANES, s), BF16),
            jax.ShapeDtypeStruct((b, ng, s, LANES), BF16),
            jax.ShapeDtypeStruct((b, n_heads, DIFF_V_DIM, s), BF16),
            jax.ShapeDtypeStruct((b, ng, 1, s), F32),
            jax.ShapeDtypeStruct((b, ng, 1, s), F32),
        ],
        compiler_params=_cparams(("parallel", "parallel")),
        name="dprep",
    )(proj3, proj3, proj3, gq, gk, cpieces)


NEG_BIG = -1e30
SAFE_LOG2_RANGE = 96.0


def _diffattn_kernel(lam_ref, qt_ref, ka_ref, vt_ref, qn2_ref, kn2_ref, gd_ref, o_ref,
                     qs_ref, corr_ref, m_ref, l_ref, acc_ref, s_refs, p_refs, bmax_refs,
                     *, tq, tk, unroll, lam_init, slopes_l2):
    h = pl.program_id(1)
    qi = pl.program_id(2)
    s_len = ka_ref.shape[1]
    nk = s_len // tk
    ratio = tq // tk
    assert tq % tk == 0 and unroll % 2 == 0 and nk % unroll == 0
    slope = jnp.float32(0.0)
    for hh, sv in enumerate(slopes_l2):
        slope = jnp.where(h == hh, jnp.float32(sv), slope)

    @pl.when(qi == 0)
    def _():
        ii = lax.broadcasted_iota(jnp.int32, (tk, tq), 0)
        jj = lax.broadcasted_iota(jnp.int32, (tk, tq), 1)
        corr_ref[0] = jnp.zeros((tk, tq), F32)
        for d in range(ratio):
            corr_ref[1 + d] = (2.0 * slope) * jnp.maximum(ii - jj + d * tk, 0).astype(F32)

    row = lax.broadcasted_iota(jnp.int32, (LANES, 1), 0)
    sign = jnp.where(row < DIFF_HEAD_DIM, 1.0, -1.0).astype(BF16)
    for mp in range(2):
        q = qt_ref[mp]
        qs_ref[0, mp] = q
        qs_ref[1, mp] = q * sign

    l_ref[...] = jnp.zeros(l_ref.shape, F32)
    acc_ref[...] = jnp.zeros(acc_ref.shape, F32)
    p_refs[1][...] = jnp.zeros(p_refs[1].shape, BF16)

    def raw_scores(j, mp):
        start = pl.multiple_of(j * tk, tk)
        d = j - ratio * qi
        above = (d >= ratio).astype(jnp.int32)
        overlap = jnp.where((d >= 0) & (d < ratio), d + 1, 0)
        k = ka_ref[mp, pl.ds(start, tk), :]
        return jnp.dot(k, qs_ref[above, mp], preferred_element_type=F32) - corr_ref[overlap]

    def weighted_values(j, slot):
        start = pl.multiple_of(j * tk, tk)
        return jnp.dot(vt_ref[:, pl.ds(start, tk)], p_refs[slot][...], preferred_element_type=F32)

    def run_pipeline(region):
        def body(i, c):
            for u in range(unroll):
                region(unroll * i + u, u % 2)
            return c

        lax.fori_loop(0, nk // unroll, body, 0)
        pv = weighted_values(nk - 1, 1)
        for mp in range(2):
            acc_ref[mp] += pv[:, mp * tq:(mp + 1) * tq]

    def bounded_path():
        def region(j, slot):
            pv = weighted_values(jnp.maximum(j - 1, 0), 1 - slot)
            for mp in range(2):
                p = jnp.exp2(raw_scores(j, mp))
                l_ref[mp] += jnp.sum(p, axis=0, keepdims=True)
                p_refs[slot][:, mp * tq:(mp + 1) * tq] = p.astype(BF16)
            for mp in range(2):
                acc_ref[mp] += pv[:, mp * tq:(mp + 1) * tq]

        run_pipeline(region)

    def online_max_path():
        m_ref[...] = jnp.full(m_ref.shape, NEG_BIG, F32)

        def scores(j, slot):
            for mp in range(2):
                s = raw_scores(j, mp)
                s_refs[slot][mp] = s
                bmax_refs[slot][mp] = jnp.max(s, axis=0, keepdims=True)

        def region(j, slot):
            pv = weighted_values(jnp.maximum(j - 1, 0), 1 - slot)
            alphas = []
            for mp in range(2):
                m_old = m_ref[mp]
                m_new = jnp.maximum(m_old, bmax_refs[slot][mp])
                p = jnp.exp2(s_refs[slot][mp] - m_new)
                alpha = jnp.exp2(m_old - m_new)
                l_ref[mp] = alpha * l_ref[mp] + jnp.sum(p, axis=0, keepdims=True)
                m_ref[mp] = m_new
                p_refs[slot][:, mp * tq:(mp + 1) * tq] = p.astype(BF16)
                alphas.append(alpha)
            scores(jnp.minimum(j + 1, nk - 1), 1 - slot)
            for mp in range(2):
                acc_ref[mp] = (acc_ref[mp] + pv[:, mp * tq:(mp + 1) * tq]) * alphas[mp]

        scores(0, 0)
        run_pipeline(region)

    bound2 = jnp.float32(0.0)
    for mp in range(2):
        bound2 = jnp.maximum(bound2, jnp.max(qn2_ref[mp]) * jnp.max(kn2_ref[mp]))
    lax.cond(bound2 <= SAFE_LOG2_RANGE ** 2, bounded_path, online_max_path)

    la = lam_ref[...]
    lam = (jnp.exp(jnp.sum(la[0:1] * la[1:2], axis=-1, keepdims=True))
           - jnp.exp(jnp.sum(la[2:3] * la[3:4], axis=-1, keepdims=True)) + lam_init)
    o1 = acc_ref[0] / l_ref[0]
    o2 = acc_ref[1] / l_ref[1]
    da = o1 - lam * o2
    y = da * lax.rsqrt(jnp.mean(da * da, axis=0, keepdims=True) + EPS)
    y = y * (gd_ref[...] * (1.0 - lam_init))
    o_ref[...] = y.T.astype(BF16)


def _diffattn(lam_params, qt, ka, vt, qn2, kn2, gd_col, tq, tk, unroll, lam_init, slopes_l2):
    b, ng, _, s = qt.shape
    n_heads = ng // 2
    kern = functools.partial(_diffattn_kernel, tq=tq, tk=tk, unroll=unroll, lam_init=lam_init,
                             slopes_l2=slopes_l2)
    return pl.pallas_call(
        kern,
        grid=(b, n_heads, s // tq),
        in_specs=[
            pl.BlockSpec((4, DIFF_HEAD_DIM), lambda i, h, q: (0, 0)),
            pl.BlockSpec((None, 2, LANES, tq), lambda i, h, q: (i, h, 0, q)),
            pl.BlockSpec((None, 2, s, LANES), lambda i, h, q: (i, h, 0, 0)),
            pl.BlockSpec((None, None, DIFF_V_DIM, s), lambda i, h, q: (i, h, 0, 0)),
            pl.BlockSpec((None, 2, 1, tq), lambda i, h, q: (i, h, 0, q)),
            pl.BlockSpec((None, 2, 1, s), lambda i, h, q: (i, h, 0, 0)),
            pl.BlockSpec((DIFF_V_DIM, 1), lambda i, h, q: (0, 0)),
        ],
        out_specs=pl.BlockSpec((None, tq, DIFF_V_DIM), lambda i, h, q: (i, q, h)),
        out_shape=jax.ShapeDtypeStruct((b, s, n_heads * DIFF_V_DIM), BF16),
        scratch_shapes=[
            pltpu.VMEM((2, 2, LANES, tq), BF16),
            pltpu.VMEM((1 + tq // tk, tk, tq), F32),
            pltpu.VMEM((2, 1, tq), F32),
            pltpu.VMEM((2, 1, tq), F32),
            pltpu.VMEM((2, DIFF_V_DIM, tq), F32),
            [pltpu.VMEM((2, tk, tq), F32)] * 2,
            [pltpu.VMEM((tk, 2 * tq), BF16)] * 2,
            [pltpu.VMEM((2, 1, tq), F32)] * 2,
        ],
        compiler_params=_cparams(("parallel", "parallel", "arbitrary")),
        name="diffattn",
    )(lam_params, qt, ka, vt, qn2, kn2, gd_col)


def _retention_kernel(dec_ref, q_ref, k_ref, v_ref, g_ref, gn_ref, o_ref,
                      kvf_ref, kvb_ref, rf_ref, rb_ref, a_ref, *, chunk, group):
    j = pl.program_id(1)
    s_len = q_ref.shape[0]
    n_chunks = s_len // chunk
    hd = RET_HEAD_DIM
    c = chunk

    lane = lax.broadcasted_iota(jnp.int32, (1, LANES), 1)
    rowi = lax.broadcasted_iota(jnp.int32, (LANES, 1), 0)
    first_l = lane < hd
    first_r = rowi < hd

    def per_lane(d):
        return -jnp.exp(jnp.where(first_l, dec_ref[d, 2 * j], dec_ref[d, 2 * j + 1]))

    def per_row(d):
        return -jnp.exp(jnp.where(first_r, dec_ref[d, 2 * j], dec_ref[d, 2 * j + 1]))

    lgf_l, lgb_l = per_lane(0), per_lane(1)
    lgf_r, lgb_r = per_row(0), per_row(1)

    pos_r = lax.broadcasted_iota(jnp.int32, (c, 1), 0).astype(F32)
    wk_f = jnp.exp(lgf_l * (c - 1.0 - pos_r))
    wq_f = jnp.exp(lgf_l * (pos_r + 1.0))
    wk_b = jnp.exp(lgb_l * pos_r)
    wq_b = jnp.exp(lgb_l * (c - pos_r))
    dc_f = jnp.exp(lgf_r * float(c))
    dc_b = jnp.exp(lgb_r * float(c))

    tt = lax.broadcasted_iota(jnp.int32, (c, c), 0)
    ss = lax.broadcasted_iota(jnp.int32, (c, c), 1)
    dist = (tt - ss).astype(F32)

    def dmask(hh):
        lf = -jnp.exp(jnp.full((1, 1), dec_ref[0, 2 * j + hh], F32))
        lb = -jnp.exp(jnp.full((1, 1), dec_ref[1, 2 * j + hh], F32))
        return jnp.where(dist >= 0, jnp.exp(lf * jnp.maximum(dist, 0.0)),
                         jnp.exp(lb * jnp.maximum(-dist, 0.0)))

    kscale = RET_HEAD_DIM ** -0.5
    dcat = jnp.concatenate([dmask(0), dmask(1)], axis=1) * kscale
    wk_f = wk_f * kscale
    wk_b = wk_b * kscale
    wq_f = wq_f.astype(BF16)
    wq_b = wq_b.astype(BF16)
    bd = (first_r == first_l).astype(F32)
    m0f = first_l.astype(F32)
    m0 = m0f.astype(BF16)
    m1 = (1.0 - m0f).astype(BF16)

    tn = (((0,), (0,)), ((), ()))
    nt = (((1,), (1,)), ((), ()))
    assert n_chunks % group == 0

    def chunk_local(i, carry):
        for u in range(group):
            n = i * group + u
            r0 = pl.multiple_of(n * c, c)
            qb = q_ref[pl.ds(r0, c), :]
            kb = k_ref[pl.ds(r0, c), :]
            vb = v_ref[pl.ds(r0, c), :]
            kf = kb.astype(F32)
            kvf_ref[n] = lax.dot_general((kf * wk_f).astype(BF16), vb, tn, preferred_element_type=F32) * bd
            kvb_ref[n] = lax.dot_general((kf * wk_b).astype(BF16), vb, tn, preferred_element_type=F32) * bd
            kcat = jnp.concatenate([kb * m0, kb * m1], axis=0)
            s = lax.dot_general(qb, kcat, nt, preferred_element_type=F32) * dcat
            a_ref[n] = jnp.concatenate([s.astype(BF16), qb * wq_f, qb * wq_b], axis=1)
        return carry

    lax.fori_loop(0, n_chunks // group, chunk_local, 0)

    def scan(n, carry):
        rf, rb = carry
        nb = n_chunks - 1 - n
        rf_ref[n] = rf.astype(BF16)
        rb_ref[nb] = rb.astype(BF16)
        return dc_f * rf + kvf_ref[n], dc_b * rb + kvb_ref[nb]

    zero = jnp.zeros((LANES, LANES), F32)
    lax.fori_loop(0, n_chunks, scan, (zero, zero))

    gn = gn_ref[...]

    def outputs(i, carry):
        for u in range(group):
            n = i * group + u
            r0 = pl.multiple_of(n * c, c)
            vb = v_ref[pl.ds(r0, c), :]
            bm = jnp.concatenate([vb * m0, vb * m1, rf_ref[n], rb_ref[n]], axis=0)
            o = jnp.dot(a_ref[n], bm, preferred_element_type=F32)
            o2 = o * o
            ms0 = jnp.sum(o2 * m0f, axis=-1, keepdims=True)
            ms1 = jnp.sum(o2 * (1.0 - m0f), axis=-1, keepdims=True)
            ms = jnp.where(first_l, ms0, ms1) * (1.0 / hd)
            y = o * lax.rsqrt(ms + EPS) * gn
            gate = g_ref[pl.ds(r0, c), :].astype(F32)
            y = y * (gate * jax.nn.sigmoid(gate))
            o_ref[pl.ds(r0, c), :] = y.astype(BF16)
        return carry

    lax.fori_loop(0, n_chunks // group, outputs, 0)


def _retention(dec, proj3, gn_lanes, ret_width, chunk, group=8):
    b, s, _ = proj3.shape
    npair = ret_width // LANES
    assert chunk == LANES
    blk = lambda o: pl.BlockSpec((None, s, LANES), lambda i, j, o=o: (i, 0, o * npair + j))
    return pl.pallas_call(
        functools.partial(_retention_kernel, chunk=chunk, group=group),
        grid=(b, npair),
        in_specs=[
            pl.BlockSpec(memory_space=pltpu.SMEM),
            blk(0), blk(1), blk(2), blk(3),
            pl.BlockSpec((1, LANES), lambda i, j: (0, 0)),
        ],
        out_specs=pl.BlockSpec((None, s, LANES), lambda i, j: (i, 0, j)),
        out_shape=jax.ShapeDtypeStruct((b, s, ret_width), BF16),
        scratch_shapes=(
            [pltpu.VMEM((s // chunk, LANES, LANES), F32)] * 2
            + [pltpu.VMEM((s // chunk, LANES, LANES), BF16)] * 2
            + [pltpu.VMEM((s // chunk, chunk, 4 * chunk), BF16)]
        ),
        compiler_params=_cparams(("parallel", "parallel")),
        name="retention",
    )(dec, proj3, proj3, proj3, proj3, gn_lanes)


def _post_kernel(x_ref, r_ref, a_ref, wo_ref, g_ref, wg_ref, wu_ref, wd_ref, o_ref, *, f_chunks):
    mix = jnp.concatenate([r_ref[...], a_ref[...]], axis=1)
    x1 = x_ref[...] + jnp.dot(mix, wo_ref[...], preferred_element_type=F32)
    h = (x1 * lax.rsqrt(jnp.mean(x1 * x1, axis=-1, keepdims=True) + EPS) * g_ref[...]).astype(BF16)
    acc = x1
    for f0, f1 in f_chunks:
        gate = jnp.dot(h, wg_ref[:, f0:f1], preferred_element_type=F32)
        up = jnp.dot(h, wu_ref[:, f0:f1], preferred_element_type=F32)
        act = (gate * jax.nn.sigmoid(gate) * up).astype(BF16)
        acc = acc + jnp.dot(act, wd_ref[f0:f1, :], preferred_element_type=F32)
    o_ref[...] = acc


def _f_chunks(d_ff, mxu_cols=256, max_cols=1536):
    out, f0 = [], 0
    while f0 < d_ff:
        f1 = min(d_ff, f0 + max_cols)
        out.append((f0, f1))
        f0 = f1
    assert all((a % mxu_cols == 0) for a, _ in out)
    return tuple(out)


def _post(x2, ret2, da2, wo, g, wg, wu, wd, tm):
    m, d = x2.shape
    rw, aw = ret2.shape[1], da2.shape[1]
    d_ff = wg.shape[1]
    const = lambda shape: pl.BlockSpec(shape, lambda i: (0, 0), pipeline_mode=pl.Buffered(1))
    return pl.pallas_call(
        functools.partial(_post_kernel, f_chunks=_f_chunks(d_ff)),
        grid=(m // tm,),
        in_specs=[
            pl.BlockSpec((tm, d), lambda i: (i, 0)),
            pl.BlockSpec((tm, rw), lambda i: (i, 0)),
            pl.BlockSpec((tm, aw), lambda i: (i, 0)),
            const((rw + aw, d)),
            const((1, d)),
            const((d, d_ff)),
            const((d, d_ff)),
            const((d_ff, d)),
        ],
        out_specs=pl.BlockSpec((tm, d), lambda i: (i, 0)),
        out_shape=jax.ShapeDtypeStruct((m, d), F32),
        compiler_params=_cparams(("parallel",)),
        name="post",
    )(x2, ret2, da2, wo, g, wg, wu, wd)


def _slope_pieces(n_heads):
    slopes = (2.0 ** (-8.0 * np.arange(1, n_heads + 1, dtype=np.float64) / n_heads) * LOG2E).astype(np.float32)
    cols = np.zeros((n_heads, 2, DIFF_HEAD_DIM, 1), np.float32)
    rem = slopes.astype(np.float64)
    for p in range(N_PIECES):
        piece = rem.astype(BF16).astype(np.float64)
        for dgt in range(N_DIGITS):
            cols[:, 0, N_DIGITS * p + dgt, 0] = piece
            cols[:, 1, N_AUG // 2 + N_DIGITS * p + dgt, 0] = piece
        rem = rem - piece
    return tuple(float(v) for v in slopes), jnp.asarray(cols)


def kernel(x, attn_norm_g, w_in, ret_decay_fwd, ret_decay_bwd, ret_norm_g, dq_norm_g, dk_norm_g,
           lambda_q1, lambda_k1, lambda_q2, lambda_k2, diff_norm_g, w_out, ffn_norm_g,
           w_gate, w_up, w_down):
    b, s, d = x.shape
    depth = w_in.shape[0]
    ret_width = d // 2
    diff_width = d - ret_width
    n_dheads = diff_width // DIFF_V_DIM
    tm = 512
    tp = 512
    tq = min(1024, s)
    tk = 512
    unroll = 4
    chunk = 128
    slopes_l2, cpieces = _slope_pieces(n_dheads)

    x2 = x.reshape(b * s, d)
    for l in range(depth):
        lam_init = 0.8 - 0.6 * math.exp(-0.3 * l)
        proj = _in_proj(x2, attn_norm_g[l][None], w_in[l].astype(BF16), tm)
        proj3 = proj.reshape(b, s, -1)

        dec = jnp.stack([ret_decay_fwd[l], ret_decay_bwd[l]]).astype(F32)
        gn_lanes = jnp.tile(ret_norm_g[l].astype(F32), LANES // RET_HEAD_DIM)[None]
        ret = _retention(dec, proj3, gn_lanes, ret_width, chunk)

        qt, ka, vt, qn2, kn2 = _dprep(proj3, dq_norm_g[l].astype(F32)[:, None],
                                      dk_norm_g[l].astype(F32)[:, None], cpieces, n_dheads, tp, 4 * ret_width)
        lam_params = jnp.stack([lambda_q1[l], lambda_k1[l], lambda_q2[l], lambda_k2[l]]).astype(F32)
        da = _diffattn(lam_params, qt, ka, vt, qn2, kn2, diff_norm_g[l].astype(F32)[:, None],
                       tq, tk, unroll, lam_init, slopes_l2)

        x2 = _post(x2, ret.reshape(b * s, ret_width), da.reshape(b * s, diff_width),
                   w_out[l].astype(BF16), ffn_norm_g[l][None],
                   w_gate[l].astype(BF16), w_up[l].astype(BF16), w_down[l].astype(BF16), tm)
    return x2.reshape(b, s, d)
```

```python
import functools
import math

import numpy as np
import jax
import jax.numpy as jnp
from jax import lax
from jax.experimental import pallas as pl
from jax.experimental.pallas import tpu as pltpu

EPS = 1e-6
LOG2E = 1.4426950408889634

RET_HEAD_DIM = 64
DIFF_HEAD_DIM = 64
DIFF_V_DIM = 128
LANES = 128
VMEM_LIMIT = 56 * 1024 * 1024

BF16 = jnp.bfloat16
F32 = jnp.float32


def _cparams(sem):
    return pltpu.CompilerParams(dimension_semantics=sem, vmem_limit_bytes=VMEM_LIMIT)


N_DIGITS = 3
N_PIECES = 3
N_AUG = 2 * N_DIGITS * N_PIECES
POS_RADIX = 32
CAST_COLS = 512


def _pos_digits(pos):
    d0 = pos % POS_RADIX
    d1 = (pos // POS_RADIX) % POS_RADIX * POS_RADIX
    d2 = pos // (POS_RADIX * POS_RADIX) * (POS_RADIX * POS_RADIX)
    return [d.astype(F32) for d in (d0, d1, d2)]


def _in_proj_kernel(x_ref, g_ref, w_ref, gq_ref, gk_ref, cp_ref,
                    pr_ref, qt_ref, ka_ref, vt_ref, qn2_ref, kn2_ref, wb_ref, *, n_heads, ret_cols):
    ts = x_ref.shape[0]
    ng = 2 * n_heads
    hd = DIFF_HEAD_DIM
    half = N_AUG // 2
    aw = n_heads * DIFF_V_DIM

    @pl.when((pl.program_id(0) == 0) & (pl.program_id(1) == 0))
    def _():
        for c0 in range(0, w_ref.shape[1], CAST_COLS):
            wb_ref[:, c0:c0 + CAST_COLS] = w_ref[:, c0:c0 + CAST_COLS].astype(BF16)

    x = x_ref[...]
    y = (x * lax.rsqrt(jnp.mean(x * x, axis=-1, keepdims=True) + EPS) * g_ref[...]).astype(BF16)
    pr_ref[...] = jnp.dot(y, wb_ref[:, :ret_cols], preferred_element_type=F32).astype(BF16)
    dqkv = jnp.dot(y, wb_ref[:, ret_cols:], preferred_element_type=F32)

    def head_norm_t(xf, g_col, scale):
        xt = xf.T.reshape(ng, hd, ts)
        ms = jnp.mean(xt * xt, axis=1, keepdims=True)
        return (xt * lax.rsqrt(ms + EPS) * (g_col[None] * scale)).astype(BF16)

    qn = head_norm_t(dqkv[:, :aw], gq_ref[...], DIFF_HEAD_DIM ** -0.5 * LOG2E)
    kn = head_norm_t(dqkv[:, aw:2 * aw], gk_ref[...], 1.0)
    qn2_ref[...] = jnp.sum(jnp.square(qn.astype(F32)), axis=1, keepdims=True)
    kn2_ref[...] = jnp.sum(jnp.square(kn.astype(F32)), axis=1, keepdims=True)

    pos = pl.program_id(1) * ts + lax.broadcasted_iota(jnp.int32, (1, ts), 1)
    d0, d1, d2 = _pos_digits(pos)
    r = lax.broadcasted_iota(jnp.int32, (hd, 1), 0)
    which = jnp.where(r < half, r, r - half) % N_DIGITS
    digit = jnp.where(which == 0, d0, jnp.where(which == 1, d1, d2))
    k_digits = jnp.where(r < half, digit, 0.0)
    q_digits = jnp.where((r >= half) & (r < N_AUG), -digit, 0.0)
    for g in range(ng):
        q_rows = (q_digits + cp_ref[g // 2, 0]).astype(BF16)
        k_rows = (k_digits + cp_ref[g // 2, 1]).astype(BF16)
        qt_ref[g] = jnp.concatenate([qn[g], q_rows], axis=0)
        ka_ref[g] = jnp.concatenate([kn[g], k_rows], axis=0).astype(F32).T.astype(BF16)
    vt_ref[...] = dqkv[:, 2 * aw:].T.reshape(n_heads, DIFF_V_DIM, ts).astype(BF16)


def _in_proj(x3, g, w, gq, gk, cpieces, n_heads, ret_cols, ts):
    b, s, d = x3.shape
    n = w.shape[1]
    ng = 2 * n_heads
    assert n == ret_cols + 3 * n_heads * DIFF_V_DIM and n % CAST_COLS == 0
    const = lambda shape: pl.BlockSpec(shape, lambda i, j: (0,) * len(shape), pipeline_mode=pl.Buffered(1))
    return pl.pallas_call(
        functools.partial(_in_proj_kernel, n_heads=n_heads, ret_cols=ret_cols),
        grid=(b, s // ts),
        in_specs=[
            pl.BlockSpec((None, ts, d), lambda i, j: (i, j, 0)),
            const((1, d)),
            const((d, n)),
            const((DIFF_HEAD_DIM, 1)),
            const((DIFF_HEAD_DIM, 1)),
            const((n_heads, 2, DIFF_HEAD_DIM, 1)),
        ],
        out_specs=[
            pl.BlockSpec((None, ts, ret_cols), lambda i, j: (i, j, 0)),
            pl.BlockSpec((None, ng, LANES, ts), lambda i, j: (i, 0, 0, j)),
            pl.BlockSpec((None, ng, ts, LANES), lambda i, j: (i, 0, j, 0)),
            pl.BlockSpec((None, n_heads, DIFF_V_DIM, ts), lambda i, j: (i, 0, 0, j)),
            pl.BlockSpec((None, ng, 1, ts), lambda i, j: (i, 0, 0, j)),
            pl.BlockSpec((None, ng, 1, ts), lambda i, j: (i, 0, 0, j)),
        ],
        out_shape=[
            jax.ShapeDtypeStruct((b, s, ret_cols), BF16),
            jax.ShapeDtypeStruct((b, ng, LANES, s), BF16),
            jax.ShapeDtypeStruct((b, ng, s, LANES), BF16),
            jax.ShapeDtypeStruct((b, n_heads, DIFF_V_DIM, s), BF16),
            jax.ShapeDtypeStruct((b, ng, 1, s), F32),
            jax.ShapeDtypeStruct((b, ng, 1, s), F32),
        ],
        scratch_shapes=[pltpu.VMEM((d, n), BF16)],
        compiler_params=_cparams(("arbitrary", "arbitrary")),
        name="in_proj",
    )(x3, g, w, gq, gk, cpieces)


NEG_BIG = -1e30
SAFE_LOG2_RANGE = 96.0


def _diffattn_kernel(lam_ref, qt_ref, ka_ref, vt_ref, qn2_ref, kn2_ref, gd_ref, o_ref,
                     qs_ref, corr_ref, m_ref, l_ref, acc_ref, s_refs, p_refs, bmax_refs, kmax_ref,
                     *, tq, tk, unroll, lam_init, slopes_l2):
    h = pl.program_id(1)
    qi = pl.program_id(2)
    s_len = ka_ref.shape[1]
    nk = s_len // tk
    ratio = tq // tk
    assert tq % tk == 0 and unroll % 2 == 0 and nk % unroll == 0
    slope = jnp.float32(0.0)
    for hh, sv in enumerate(slopes_l2):
        slope = jnp.where(h == hh, jnp.float32(sv), slope)

    @pl.when(qi == 0)
    def _():
        ii = lax.broadcasted_iota(jnp.int32, (tk, tq), 0)
        jj = lax.broadcasted_iota(jnp.int32, (tk, tq), 1)
        corr_ref[0] = jnp.zeros((tk, tq), F32)
        for d in range(ratio):
            corr_ref[1 + d] = (2.0 * slope) * jnp.maximum(ii - jj + d * tk, 0).astype(F32)
        for mp in range(2):
            kmax_ref[mp] = jnp.max(kn2_ref[mp])

    row = lax.broadcasted_iota(jnp.int32, (LANES, 1), 0)
    sign = jnp.where(row < DIFF_HEAD_DIM, 1.0, -1.0).astype(BF16)
    for mp in range(2):
        q = qt_ref[mp]
        qs_ref[0, mp] = q
        qs_ref[1, mp] = q * sign

    l_ref[...] = jnp.zeros(l_ref.shape, F32)
    acc_ref[...] = jnp.zeros(acc_ref.shape, F32)
    p_refs[1][...] = jnp.zeros(p_refs[1].shape, BF16)

    def raw_scores(j, mp):
        start = pl.multiple_of(j * tk, tk)
        d = j - ratio * qi
        above = (d >= ratio).astype(jnp.int32)
        overlap = jnp.where((d >= 0) & (d < ratio), d + 1, 0)
        k = ka_ref[mp, pl.ds(start, tk), :]
        return jnp.dot(k, qs_ref[above, mp], preferred_element_type=F32) - corr_ref[overlap]

    def weighted_values(j, slot):
        start = pl.multiple_of(j * tk, tk)
        return jnp.dot(vt_ref[:, pl.ds(start, tk)], p_refs[slot][...], preferred_element_type=F32)

    def run_pipeline(region, blocks_per_iter):
        def body(i, c):
            for u in range(blocks_per_iter):
                region(blocks_per_iter * i + u, u % 2)
            return c

        lax.fori_loop(0, nk // blocks_per_iter, body, 0)
        pv = weighted_values(nk - 1, 1)
        for mp in range(2):
            acc_ref[mp] += pv[:, mp * tq:(mp + 1) * tq]

    def bounded_path():
        def region(j, slot):
            pv = weighted_values(jnp.maximum(j - 1, 0), 1 - slot)
            for mp in range(2):
                p = jnp.exp2(raw_scores(j, mp))
                l_ref[mp] += jnp.sum(p, axis=0, keepdims=True)
                p_refs[slot][:, mp * tq:(mp + 1) * tq] = p.astype(BF16)
            for mp in range(2):
                acc_ref[mp] += pv[:, mp * tq:(mp + 1) * tq]

        run_pipeline(region, unroll)

    def online_max_path():
        m_ref[...] = jnp.full(m_ref.shape, NEG_BIG, F32)

        def scores(j, slot):
            for mp in range(2):
                s = raw_scores(j, mp)
                s_refs[slot][mp] = s
                bmax_refs[slot][mp] = jnp.max(s, axis=0, keepdims=True)

        def region(j, slot):
            pv = weighted_values(jnp.maximum(j - 1, 0), 1 - slot)
            alphas = []
            for mp in range(2):
                m_old = m_ref[mp]
                m_new = jnp.maximum(m_old, bmax_refs[slot][mp])
                p = jnp.exp2(s_refs[slot][mp] - m_new)
                alpha = jnp.exp2(m_old - m_new)
                l_ref[mp] = alpha * l_ref[mp] + jnp.sum(p, axis=0, keepdims=True)
                m_ref[mp] = m_new
                p_refs[slot][:, mp * tq:(mp + 1) * tq] = p.astype(BF16)
                alphas.append(alpha)
            scores(jnp.minimum(j + 1, nk - 1), 1 - slot)
            for mp in range(2):
                acc_ref[mp] = (acc_ref[mp] + pv[:, mp * tq:(mp + 1) * tq]) * alphas[mp]

        scores(0, 0)
        run_pipeline(region, 2)

    bound2 = jnp.float32(0.0)
    for mp in range(2):
        bound2 = jnp.maximum(bound2, jnp.max(qn2_ref[mp]) * kmax_ref[mp])
    lax.cond(bound2 <= SAFE_LOG2_RANGE ** 2, bounded_path, online_max_path)

    la = lam_ref[...]
    lam = (jnp.exp(jnp.sum(la[0:1] * la[1:2], axis=-1, keepdims=True))
           - jnp.exp(jnp.sum(la[2:3] * la[3:4], axis=-1, keepdims=True)) + lam_init)
    o1 = acc_ref[0] / l_ref[0]
    o2 = acc_ref[1] / l_ref[1]
    da = o1 - lam * o2
    y = da * lax.rsqrt(jnp.mean(da * da, axis=0, keepdims=True) + EPS)
    y = y * (gd_ref[...] * (1.0 - lam_init))
    o_ref[...] = y.T.astype(BF16)


def _diffattn(lam_params, qt, ka, vt, qn2, kn2, gd_col, tq, tk, unroll, lam_init, slopes_l2):
    b, ng, _, s = qt.shape
    n_heads = ng // 2
    kern = functools.partial(_diffattn_kernel, tq=tq, tk=tk, unroll=unroll, lam_init=lam_init,
                             slopes_l2=slopes_l2)
    return pl.pallas_call(
        kern,
        grid=(b, n_heads, s // tq),
        in_specs=[
            pl.BlockSpec((4, DIFF_HEAD_DIM), lambda i, h, q: (0, 0)),
            pl.BlockSpec((None, 2, LANES, tq), lambda i, h, q: (i, h, 0, q)),
            pl.BlockSpec((None, 2, s, LANES), lambda i, h, q: (i, h, 0, 0)),
            pl.BlockSpec((None, None, DIFF_V_DIM, s), lambda i, h, q: (i, h, 0, 0)),
            pl.BlockSpec((None, 2, 1, tq), lambda i, h, q: (i, h, 0, q)),
            pl.BlockSpec((None, 2, 1, s), lambda i, h, q: (i, h, 0, 0)),
            pl.BlockSpec((DIFF_V_DIM, 1), lambda i, h, q: (0, 0)),
        ],
        out_specs=pl.BlockSpec((None, tq, DIFF_V_DIM), lambda i, h, q: (i, q, h)),
        out_shape=jax.ShapeDtypeStruct((b, s, n_heads * DIFF_V_DIM), BF16),
        scratch_shapes=[
            pltpu.VMEM((2, 2, LANES, tq), BF16),
            pltpu.VMEM((1 + tq // tk, tk, tq), F32),
            pltpu.VMEM((2, 1, tq), F32),
            pltpu.VMEM((2, 1, tq), F32),
            pltpu.VMEM((2, DIFF_V_DIM, tq), F32),
            [pltpu.VMEM((2, tk, tq), F32)] * 2,
            [pltpu.VMEM((tk, 2 * tq), BF16)] * 2,
            [pltpu.VMEM((2, 1, tq), F32)] * 2,
            pltpu.SMEM((2,), F32),
        ],
        compiler_params=_cparams(("parallel", "parallel", "arbitrary")),
        name="diffattn",
    )(lam_params, qt, ka, vt, qn2, kn2, gd_col)


def _retention_kernel(dec_ref, q_ref, k_ref, v_ref, g_ref, gn_ref, o_ref,
                      kvf_ref, kvb_ref, rf_ref, rb_ref, a_ref, *, chunk, group):
    j = pl.program_id(1)
    s_len = q_ref.shape[0]
    n_chunks = s_len // chunk
    hd = RET_HEAD_DIM
    c = chunk

    lane = lax.broadcasted_iota(jnp.int32, (1, LANES), 1)
    rowi = lax.broadcasted_iota(jnp.int32, (LANES, 1), 0)
    first_l = lane < hd
    first_r = rowi < hd

    def per_lane(d):
        return -jnp.exp(jnp.where(first_l, dec_ref[d, 2 * j], dec_ref[d, 2 * j + 1]))

    def per_row(d):
        return -jnp.exp(jnp.where(first_r, dec_ref[d, 2 * j], dec_ref[d, 2 * j + 1]))

    lgf_l, lgb_l = per_lane(0), per_lane(1)
    lgf_r, lgb_r = per_row(0), per_row(1)

    pos_r = lax.broadcasted_iota(jnp.int32, (c, 1), 0).astype(F32)
    wk_f = jnp.exp(lgf_l * (c - 1.0 - pos_r))
    wq_f = jnp.exp(lgf_l * (pos_r + 1.0))
    wk_b = jnp.exp(lgb_l * pos_r)
    wq_b = jnp.exp(lgb_l * (c - pos_r))
    dc_f = jnp.exp(lgf_r * float(c))
    dc_b = jnp.exp(lgb_r * float(c))

    tt = lax.broadcasted_iota(jnp.int32, (c, c), 0)
    ss = lax.broadcasted_iota(jnp.int32, (c, c), 1)
    dist = (tt - ss).astype(F32)

    def dmask(hh):
        lf = -jnp.exp(jnp.full((1, 1), dec_ref[0, 2 * j + hh], F32))
        lb = -jnp.exp(jnp.full((1, 1), dec_ref[1, 2 * j + hh], F32))
        return jnp.where(dist >= 0, jnp.exp(lf * jnp.maximum(dist, 0.0)),
                         jnp.exp(lb * jnp.maximum(-dist, 0.0)))

    kscale = RET_HEAD_DIM ** -0.5
    dcat = jnp.concatenate([dmask(0), dmask(1)], axis=1) * kscale
    wk_f = wk_f * kscale
    wk_b = wk_b * kscale
    wq_f = wq_f.astype(BF16)
    wq_b = wq_b.astype(BF16)
    bd = (first_r == first_l).astype(F32)
    m0f = first_l.astype(F32)
    m0 = m0f.astype(BF16)
    m1 = (1.0 - m0f).astype(BF16)

    tn = (((0,), (0,)), ((), ()))
    nt = (((1,), (1,)), ((), ()))
    assert n_chunks % group == 0

    def chunk_local(i, carry):
        for u in range(group):
            n = i * group + u
            r0 = pl.multiple_of(n * c, c)
            qb = q_ref[pl.ds(r0, c), :]
            kb = k_ref[pl.ds(r0, c), :]
            vb = v_ref[pl.ds(r0, c), :]
            kf = kb.astype(F32)
            kvf_ref[n] = lax.dot_general((kf * wk_f).astype(BF16), vb, tn, preferred_element_type=F32) * bd
            kvb_ref[n] = lax.dot_general((kf * wk_b).astype(BF16), vb, tn, preferred_element_type=F32) * bd
            kcat = jnp.concatenate([kb * m0, kb * m1], axis=0)
            s = lax.dot_general(qb, kcat, nt, preferred_element_type=F32) * dcat
            a_ref[n] = jnp.concatenate([s.astype(BF16), qb * wq_f, qb * wq_b], axis=1)
        return carry

    lax.fori_loop(0, n_chunks // group, chunk_local, 0)

    def scan(n, carry):
        rf, rb = carry
        nb = n_chunks - 1 - n
        rf_ref[n] = rf.astype(BF16)
        rb_ref[nb] = rb.astype(BF16)
        return dc_f * rf + kvf_ref[n], dc_b * rb + kvb_ref[nb]

    zero = jnp.zeros((LANES, LANES), F32)
    lax.fori_loop(0, n_chunks, scan, (zero, zero))

    gn = gn_ref[...]

    def outputs(i, carry):
        for u in range(group):
            n = i * group + u
            r0 = pl.multiple_of(n * c, c)
            vb = v_ref[pl.ds(r0, c), :]
            bm = jnp.concatenate([vb * m0, vb * m1, rf_ref[n], rb_ref[n]], axis=0)
            o = jnp.dot(a_ref[n], bm, preferred_element_type=F32)
            o2 = o * o
            ms0 = jnp.sum(o2 * m0f, axis=-1, keepdims=True)
            ms1 = jnp.sum(o2 * (1.0 - m0f), axis=-1, keepdims=True)
            ms = jnp.where(first_l, ms0, ms1) * (1.0 / hd)
            y = o * lax.rsqrt(ms + EPS) * gn
            gate = g_ref[pl.ds(r0, c), :].astype(F32)
            y = y * (gate * jax.nn.sigmoid(gate))
            o_ref[pl.ds(r0, c), :] = y.astype(BF16)
        return carry

    lax.fori_loop(0, n_chunks // group, outputs, 0)


def _retention(dec, proj3, gn_lanes, ret_width, chunk, group=8):
    b, s, _ = proj3.shape
    npair = ret_width // LANES
    assert chunk == LANES
    blk = lambda o: pl.BlockSpec((None, s, LANES), lambda i, j, o=o: (i, 0, o * npair + j))
    return pl.pallas_call(
        functools.partial(_retention_kernel, chunk=chunk, group=group),
        grid=(b, npair),
        in_specs=[
            pl.BlockSpec(memory_space=pltpu.SMEM),
            blk(0), blk(1), blk(2), blk(3),
            pl.BlockSpec((1, LANES), lambda i, j: (0, 0)),
        ],
        out_specs=pl.BlockSpec((None, s, LANES), lambda i, j: (i, 0, j)),
        out_shape=jax.ShapeDtypeStruct((b, s, ret_width), BF16),
        scratch_shapes=(
            [pltpu.VMEM((s // chunk, LANES, LANES), F32)] * 2
            + [pltpu.VMEM((s // chunk, LANES, LANES), BF16)] * 2
            + [pltpu.VMEM((s // chunk, chunk, 4 * chunk), BF16)]
        ),
        compiler_params=_cparams(("parallel", "parallel")),
        name="retention",
    )(dec, proj3, proj3, proj3, proj3, gn_lanes)


def _post_kernel(x_ref, r_ref, a_ref, wo_ref, g_ref, wg_ref, wu_ref, wd_ref, o_ref, *, f_chunks):
    mix = jnp.concatenate([r_ref[...], a_ref[...]], axis=1)
    x1 = x_ref[...] + jnp.dot(mix, wo_ref[...], preferred_element_type=F32)
    h = (x1 * lax.rsqrt(jnp.mean(x1 * x1, axis=-1, keepdims=True) + EPS) * g_ref[...]).astype(BF16)
    acc = x1
    for f0, f1 in f_chunks:
        gate = jnp.dot(h, wg_ref[:, f0:f1], preferred_element_type=F32)
        up = jnp.dot(h, wu_ref[:, f0:f1], preferred_element_type=F32)
        act = (gate * jax.nn.sigmoid(gate) * up).astype(BF16)
        acc = acc + jnp.dot(act, wd_ref[f0:f1, :], preferred_element_type=F32)
    o_ref[...] = acc


def _f_chunks(d_ff, mxu_cols=256, max_cols=1536):
    out, f0 = [], 0
    while f0 < d_ff:
        f1 = min(d_ff, f0 + max_cols)
        out.append((f0, f1))
        f0 = f1
    assert all((a % mxu_cols == 0) for a, _ in out)
    return tuple(out)


def _post(x2, ret2, da2, wo, g, wg, wu, wd, tm):
    m, d = x2.shape
    rw, aw = ret2.shape[1], da2.shape[1]
    d_ff = wg.shape[1]
    const = lambda shape: pl.BlockSpec(shape, lambda i: (0, 0), pipeline_mode=pl.Buffered(1))
    return pl.pallas_call(
        functools.partial(_post_kernel, f_chunks=_f_chunks(d_ff)),
        grid=(m // tm,),
        in_specs=[
            pl.BlockSpec((tm, d), lambda i: (i, 0)),
            pl.BlockSpec((tm, rw), lambda i: (i, 0)),
            pl.BlockSpec((tm, aw), lambda i: (i, 0)),
            const((rw + aw, d)),
            const((1, d)),
            const((d, d_ff)),
            const((d, d_ff)),
            const((d_ff, d)),
        ],
        out_specs=pl.BlockSpec((tm, d), lambda i: (i, 0)),
        out_shape=jax.ShapeDtypeStruct((m, d), F32),
        compiler_params=_cparams(("parallel",)),
        name="post",
    )(x2, ret2, da2, wo, g, wg, wu, wd)


def _slope_pieces(n_heads):
    slopes = (2.0 ** (-8.0 * np.arange(1, n_heads + 1, dtype=np.float64) / n_heads) * LOG2E).astype(np.float32)
    cols = np.zeros((n_heads, 2, DIFF_HEAD_DIM, 1), np.float32)
    rem = slopes.astype(np.float64)
    for p in range(N_PIECES):
        piece = rem.astype(BF16).astype(np.float64)
        for dgt in range(N_DIGITS):
            cols[:, 0, N_DIGITS * p + dgt, 0] = piece
            cols[:, 1, N_AUG // 2 + N_DIGITS * p + dgt, 0] = piece
        rem = rem - piece
    return tuple(float(v) for v in slopes), jnp.asarray(cols)


def kernel(x, attn_norm_g, w_in, ret_decay_fwd, ret_decay_bwd, ret_norm_g, dq_norm_g, dk_norm_g,
           lambda_q1, lambda_k1, lambda_q2, lambda_k2, diff_norm_g, w_out, ffn_norm_g,
           w_gate, w_up, w_down):
    b, s, d = x.shape
    depth = w_in.shape[0]
    ret_width = d // 2
    diff_width = d - ret_width
    n_dheads = diff_width // DIFF_V_DIM
    tm = 512
    tq = min(1024, s)
    tk = 512
    unroll = 8
    chunk = 128
    slopes_l2, cpieces = _slope_pieces(n_dheads)

    x2 = x.reshape(b * s, d)
    for l in range(depth):
        lam_init = 0.8 - 0.6 * math.exp(-0.3 * l)
        pr, qt, ka, vt, qn2, kn2 = _in_proj(
            x2.reshape(b, s, d), attn_norm_g[l][None], w_in[l], dq_norm_g[l].astype(F32)[:, None],
            dk_norm_g[l].astype(F32)[:, None], cpieces, n_dheads, 4 * ret_width, tm)

        dec = jnp.stack([ret_decay_fwd[l], ret_decay_bwd[l]]).astype(F32)
        gn_lanes = jnp.tile(ret_norm_g[l].astype(F32), LANES // RET_HEAD_DIM)[None]
        ret = _retention(dec, pr, gn_lanes, ret_width, chunk)

        lam_params = jnp.stack([lambda_q1[l], lambda_k1[l], lambda_q2[l], lambda_k2[l]]).astype(F32)
        da = _diffattn(lam_params, qt, ka, vt, qn2, kn2, diff_norm_g[l].astype(F32)[:, None],
                       tq, tk, unroll, lam_init, slopes_l2)

        x2 = _post(x2, ret.reshape(b * s, ret_width), da.reshape(b * s, diff_width),
                   w_out[l].astype(BF16), ffn_norm_g[l][None],
                   w_gate[l].astype(BF16), w_up[l].astype(BF16), w_down[l].astype(BF16), tm)
    return x2.reshape(b, s, d)
```

```python
import functools
import math

import numpy as np
import jax
import jax.numpy as jnp
from jax import lax
from jax.experimental import pallas as pl
from jax.experimental.pallas import tpu as pltpu

EPS = 1e-6
LOG2E = 1.4426950408889634

RET_HEAD_DIM = 64
DIFF_HEAD_DIM = 64
DIFF_V_DIM = 128
LANES = 128
VMEM_LIMIT = 56 * 1024 * 1024

BF16 = jnp.bfloat16
F32 = jnp.float32


def _cparams(sem):
    return pltpu.CompilerParams(dimension_semantics=sem, vmem_limit_bytes=VMEM_LIMIT)


N_DIGITS = 3
N_PIECES = 3
N_AUG = 2 * N_DIGITS * N_PIECES
POS_RADIX = 32
CAST_COLS = 512


def _pos_digits(pos):
    d0 = pos % POS_RADIX
    d1 = (pos // POS_RADIX) % POS_RADIX * POS_RADIX
    d2 = pos // (POS_RADIX * POS_RADIX) * (POS_RADIX * POS_RADIX)
    return [d.astype(F32) for d in (d0, d1, d2)]


def _in_proj_kernel(x_ref, g_ref, w_ref, gq_ref, gk_ref, cp_ref,
                    pr_ref, qt_ref, ka_ref, vt_ref, qn2_ref, kn2_ref, wb_ref, *, n_heads, ret_cols):
    ts = x_ref.shape[0]
    ng = 2 * n_heads
    hd = DIFF_HEAD_DIM
    half = N_AUG // 2
    aw = n_heads * DIFF_V_DIM

    @pl.when((pl.program_id(0) == 0) & (pl.program_id(1) == 0))
    def _():
        for c0 in range(0, w_ref.shape[1], CAST_COLS):
            wb_ref[:, c0:c0 + CAST_COLS] = w_ref[:, c0:c0 + CAST_COLS].astype(BF16)

    x = x_ref[...]
    y = (x * lax.rsqrt(jnp.mean(x * x, axis=-1, keepdims=True) + EPS) * g_ref[...]).astype(BF16)
    pr_ref[...] = jnp.dot(y, wb_ref[:, :ret_cols], preferred_element_type=F32).astype(BF16)
    dqkv = jnp.dot(y, wb_ref[:, ret_cols:], preferred_element_type=F32)

    def head_norm_t(xf, g_col, scale):
        xt = xf.T.reshape(ng, hd, ts)
        ms = jnp.mean(xt * xt, axis=1, keepdims=True)
        return (xt * lax.rsqrt(ms + EPS) * (g_col[None] * scale)).astype(BF16)

    qn = head_norm_t(dqkv[:, :aw], gq_ref[...], DIFF_HEAD_DIM ** -0.5 * LOG2E)
    kn = head_norm_t(dqkv[:, aw:2 * aw], gk_ref[...], 1.0)
    qn2_ref[...] = jnp.sum(jnp.square(qn.astype(F32)), axis=1, keepdims=True)
    kn2_ref[...] = jnp.sum(jnp.square(kn.astype(F32)), axis=1, keepdims=True)

    pos = pl.program_id(1) * ts + lax.broadcasted_iota(jnp.int32, (1, ts), 1)
    d0, d1, d2 = _pos_digits(pos)
    r = lax.broadcasted_iota(jnp.int32, (hd, 1), 0)
    which = jnp.where(r < half, r, r - half) % N_DIGITS
    digit = jnp.where(which == 0, d0, jnp.where(which == 1, d1, d2))
    k_digits = jnp.where(r < half, digit, 0.0)
    q_digits = jnp.where((r >= half) & (r < N_AUG), -digit, 0.0)
    for g in range(ng):
        q_rows = (q_digits + cp_ref[g // 2, 0]).astype(BF16)
        k_rows = (k_digits + cp_ref[g // 2, 1]).astype(BF16)
        qt_ref[g] = jnp.concatenate([qn[g], q_rows], axis=0)
        ka_ref[g] = jnp.concatenate([kn[g], k_rows], axis=0).astype(F32).T.astype(BF16)
    vt_ref[...] = dqkv[:, 2 * aw:].T.reshape(n_heads, DIFF_V_DIM, ts).astype(BF16)


def _in_proj(x3, g, w_all, layer, gq, gk, cpieces, n_heads, ret_cols, ts):
    b, s, d = x3.shape
    n = w_all.shape[2]
    ng = 2 * n_heads
    assert n == ret_cols + 3 * n_heads * DIFF_V_DIM and n % CAST_COLS == 0
    const = lambda shape: pl.BlockSpec(shape, lambda i, j: (0,) * len(shape), pipeline_mode=pl.Buffered(1))
    return pl.pallas_call(
        functools.partial(_in_proj_kernel, n_heads=n_heads, ret_cols=ret_cols),
        grid=(b, s // ts),
        in_specs=[
            pl.BlockSpec((None, ts, d), lambda i, j: (i, j, 0)),
            const((1, d)),
            pl.BlockSpec((None, d, n), lambda i, j: (layer, 0, 0), pipeline_mode=pl.Buffered(1)),
            const((DIFF_HEAD_DIM, 1)),
            const((DIFF_HEAD_DIM, 1)),
            const((n_heads, 2, DIFF_HEAD_DIM, 1)),
        ],
        out_specs=[
            pl.BlockSpec((None, ts, ret_cols), lambda i, j: (i, j, 0)),
            pl.BlockSpec((None, ng, LANES, ts), lambda i, j: (i, 0, 0, j)),
            pl.BlockSpec((None, ng, ts, LANES), lambda i, j: (i, 0, j, 0)),
            pl.BlockSpec((None, n_heads, DIFF_V_DIM, ts), lambda i, j: (i, 0, 0, j)),
            pl.BlockSpec((None, ng, 1, ts), lambda i, j: (i, 0, 0, j)),
            pl.BlockSpec((None, ng, 1, ts), lambda i, j: (i, 0, 0, j)),
        ],
        out_shape=[
            jax.ShapeDtypeStruct((b, s, ret_cols), BF16),
            jax.ShapeDtypeStruct((b, ng, LANES, s), BF16),
            jax.ShapeDtypeStruct((b, ng, s, LANES), BF16),
            jax.ShapeDtypeStruct((b, n_heads, DIFF_V_DIM, s), BF16),
            jax.ShapeDtypeStruct((b, ng, 1, s), F32),
            jax.ShapeDtypeStruct((b, ng, 1, s), F32),
        ],
        scratch_shapes=[pltpu.VMEM((d, n), BF16)],
        compiler_params=_cparams(("arbitrary", "arbitrary")),
        name="in_proj",
    )(x3, g, w_all, gq, gk, cpieces)


NEG_BIG = -1e30
SAFE_LOG2_RANGE = 96.0


def _diffattn_kernel(lam_ref, qt_ref, ka_ref, vt_ref, qn2_ref, kn2_ref, gd_ref, o_ref,
                     qs_ref, corr_ref, m_ref, l_ref, acc_ref, s_refs, p_refs, bmax_refs, kmax_ref,
                     *, tq, tk, lam_init, slopes_l2):
    h = pl.program_id(1)
    qi = pl.program_id(2)
    s_len = ka_ref.shape[1]
    nk = s_len // tk
    ratio = tq // tk
    assert tq % tk == 0 and nk % 2 == 0
    slope = jnp.float32(0.0)
    for hh, sv in enumerate(slopes_l2):
        slope = jnp.where(h == hh, jnp.float32(sv), slope)

    @pl.when(qi == 0)
    def _():
        ii = lax.broadcasted_iota(jnp.int32, (tk, tq), 0)
        jj = lax.broadcasted_iota(jnp.int32, (tk, tq), 1)
        corr_ref[0] = jnp.zeros((tk, tq), F32)
        for d in range(ratio):
            corr_ref[1 + d] = (2.0 * slope) * jnp.maximum(ii - jj + d * tk, 0).astype(F32)
        for mp in range(2):
            kmax_ref[mp] = jnp.max(kn2_ref[mp])

    row = lax.broadcasted_iota(jnp.int32, (LANES, 1), 0)
    sign = jnp.where(row < DIFF_HEAD_DIM, 1.0, -1.0).astype(BF16)
    for mp in range(2):
        q = qt_ref[mp]
        qs_ref[0, mp] = q
        qs_ref[1, mp] = q * sign

    l_ref[...] = jnp.zeros(l_ref.shape, F32)
    acc_ref[...] = jnp.zeros(acc_ref.shape, F32)

    def block_start(j):
        return j * tk if isinstance(j, int) else pl.multiple_of(j * tk, tk)

    def raw_scores(j, mp):
        d = j - ratio * qi
        above = (d >= ratio).astype(jnp.int32)
        overlap = jnp.where((d >= 0) & (d < ratio), d + 1, 0)
        k = ka_ref[mp, pl.ds(block_start(j), tk), :]
        return jnp.dot(k, qs_ref[above, mp], preferred_element_type=F32) - corr_ref[overlap]

    def weighted_values(j, slot):
        return jnp.dot(vt_ref[:, pl.ds(block_start(j), tk)], p_refs[slot][...], preferred_element_type=F32)

    def add_values(pv):
        for mp in range(2):
            acc_ref[mp] += pv[:, mp * tq:(mp + 1) * tq]

    def bounded_path():
        for j in range(nk):
            slot = j % 2
            if j > 0:
                pv = weighted_values(j - 1, 1 - slot)
            for mp in range(2):
                p = jnp.exp2(raw_scores(j, mp))
                l_ref[mp] += jnp.sum(p, axis=0, keepdims=True)
                p_refs[slot][:, mp * tq:(mp + 1) * tq] = p.astype(BF16)
            if j > 0:
                add_values(pv)
        add_values(weighted_values(nk - 1, (nk - 1) % 2))

    def online_max_path():
        m_ref[...] = jnp.full(m_ref.shape, NEG_BIG, F32)
        p_refs[1][...] = jnp.zeros(p_refs[1].shape, BF16)

        def scores(j, slot):
            for mp in range(2):
                s = raw_scores(j, mp)
                s_refs[slot][mp] = s
                bmax_refs[slot][mp] = jnp.max(s, axis=0, keepdims=True)

        def region(j, slot):
            pv = weighted_values(jnp.maximum(j - 1, 0), 1 - slot)
            alphas = []
            for mp in range(2):
                m_old = m_ref[mp]
                m_new = jnp.maximum(m_old, bmax_refs[slot][mp])
                p = jnp.exp2(s_refs[slot][mp] - m_new)
                alpha = jnp.exp2(m_old - m_new)
                l_ref[mp] = alpha * l_ref[mp] + jnp.sum(p, axis=0, keepdims=True)
                m_ref[mp] = m_new
                p_refs[slot][:, mp * tq:(mp + 1) * tq] = p.astype(BF16)
                alphas.append(alpha)
            scores(jnp.minimum(j + 1, nk - 1), 1 - slot)
            for mp in range(2):
                acc_ref[mp] = (acc_ref[mp] + pv[:, mp * tq:(mp + 1) * tq]) * alphas[mp]

        scores(0, 0)

        def pair(i, c):
            region(2 * i, 0)
            region(2 * i + 1, 1)
            return c

        lax.fori_loop(0, nk // 2, pair, 0)
        add_values(weighted_values(nk - 1, 1))

    bound2 = jnp.float32(0.0)
    for mp in range(2):
        bound2 = jnp.maximum(bound2, jnp.max(qn2_ref[mp]) * kmax_ref[mp])
    lax.cond(bound2 <= SAFE_LOG2_RANGE ** 2, bounded_path, online_max_path)

    la = lam_ref[...]
    lam = (jnp.exp(jnp.sum(la[0:1] * la[1:2], axis=-1, keepdims=True))
           - jnp.exp(jnp.sum(la[2:3] * la[3:4], axis=-1, keepdims=True)) + lam_init)
    o1 = acc_ref[0] / l_ref[0]
    o2 = acc_ref[1] / l_ref[1]
    da = o1 - lam * o2
    y = da * lax.rsqrt(jnp.mean(da * da, axis=0, keepdims=True) + EPS)
    y = y * (gd_ref[...] * (1.0 - lam_init))
    o_ref[...] = y.T.astype(BF16)


def _diffattn(lam_params, qt, ka, vt, qn2, kn2, gd_col, tq, tk, lam_init, slopes_l2):
    b, ng, _, s = qt.shape
    n_heads = ng // 2
    kern = functools.partial(_diffattn_kernel, tq=tq, tk=tk, lam_init=lam_init, slopes_l2=slopes_l2)
    return pl.pallas_call(
        kern,
        grid=(b, n_heads, s // tq),
        in_specs=[
            pl.BlockSpec((4, DIFF_HEAD_DIM), lambda i, h, q: (0, 0)),
            pl.BlockSpec((None, 2, LANES, tq), lambda i, h, q: (i, h, 0, q)),
            pl.BlockSpec((None, 2, s, LANES), lambda i, h, q: (i, h, 0, 0)),
            pl.BlockSpec((None, None, DIFF_V_DIM, s), lambda i, h, q: (i, h, 0, 0)),
            pl.BlockSpec((None, 2, 1, tq), lambda i, h, q: (i, h, 0, q)),
            pl.BlockSpec((None, 2, 1, s), lambda i, h, q: (i, h, 0, 0)),
            pl.BlockSpec((DIFF_V_DIM, 1), lambda i, h, q: (0, 0)),
        ],
        out_specs=pl.BlockSpec((None, tq, DIFF_V_DIM), lambda i, h, q: (i, q, h)),
        out_shape=jax.ShapeDtypeStruct((b, s, n_heads * DIFF_V_DIM), BF16),
        scratch_shapes=[
            pltpu.VMEM((2, 2, LANES, tq), BF16),
            pltpu.VMEM((1 + tq // tk, tk, tq), F32),
            pltpu.VMEM((2, 1, tq), F32),
            pltpu.VMEM((2, 1, tq), F32),
            pltpu.VMEM((2, DIFF_V_DIM, tq), F32),
            [pltpu.VMEM((2, tk, tq), F32)] * 2,
            [pltpu.VMEM((tk, 2 * tq), BF16)] * 2,
            [pltpu.VMEM((2, 1, tq), F32)] * 2,
            pltpu.SMEM((2,), F32),
        ],
        compiler_params=_cparams(("parallel", "parallel", "arbitrary")),
        name="diffattn",
    )(lam_params, qt, ka, vt, qn2, kn2, gd_col)


def _retention_kernel(dec_ref, q_ref, k_ref, v_ref, g_ref, gn_ref, o_ref,
                      kvf_ref, kvb_ref, rf_ref, rb_ref, a_ref, *, chunk, group):
    j = pl.program_id(1)
    s_len = q_ref.shape[0]
    n_chunks = s_len // chunk
    hd = RET_HEAD_DIM
    c = chunk

    lane = lax.broadcasted_iota(jnp.int32, (1, LANES), 1)
    rowi = lax.broadcasted_iota(jnp.int32, (LANES, 1), 0)
    first_l = lane < hd
    first_r = rowi < hd

    def per_lane(d):
        return -jnp.exp(jnp.where(first_l, dec_ref[d, 2 * j], dec_ref[d, 2 * j + 1]))

    def per_row(d):
        return -jnp.exp(jnp.where(first_r, dec_ref[d, 2 * j], dec_ref[d, 2 * j + 1]))

    lgf_l, lgb_l = per_lane(0), per_lane(1)
    lgf_r, lgb_r = per_row(0), per_row(1)

    pos_r = lax.broadcasted_iota(jnp.int32, (c, 1), 0).astype(F32)
    wk_f = jnp.exp(lgf_l * (c - 1.0 - pos_r))
    wq_f = jnp.exp(lgf_l * (pos_r + 1.0))
    wk_b = jnp.exp(lgb_l * pos_r)
    wq_b = jnp.exp(lgb_l * (c - pos_r))
    dc_f = jnp.exp(lgf_r * float(c))
    dc_b = jnp.exp(lgb_r * float(c))

    tt = lax.broadcasted_iota(jnp.int32, (c, c), 0)
    ss = lax.broadcasted_iota(jnp.int32, (c, c), 1)
    dist = (tt - ss).astype(F32)

    def dmask(hh):
        lf = -jnp.exp(jnp.full((1, 1), dec_ref[0, 2 * j + hh], F32))
        lb = -jnp.exp(jnp.full((1, 1), dec_ref[1, 2 * j + hh], F32))
        return jnp.where(dist >= 0, jnp.exp(lf * jnp.maximum(dist, 0.0)),
                         jnp.exp(lb * jnp.maximum(-dist, 0.0)))

    kscale = RET_HEAD_DIM ** -0.5
    dcat = jnp.concatenate([dmask(0), dmask(1)], axis=1) * kscale
    wk_f = wk_f * kscale
    wk_b = wk_b * kscale
    wq_f = wq_f.astype(BF16)
    wq_b = wq_b.astype(BF16)
    bd = (first_r == first_l).astype(F32)
    m0f = first_l.astype(F32)
    m0 = m0f.astype(BF16)
    m1 = (1.0 - m0f).astype(BF16)

    tn = (((0,), (0,)), ((), ()))
    nt = (((1,), (1,)), ((), ()))
    assert n_chunks % group == 0

    def chunk_local(i, carry):
        for u in range(group):
            n = i * group + u
            r0 = pl.multiple_of(n * c, c)
            qb = q_ref[pl.ds(r0, c), :]
            kb = k_ref[pl.ds(r0, c), :]
            vb = v_ref[pl.ds(r0, c), :]
            kf = kb.astype(F32)
            kvf_ref[n] = lax.dot_general((kf * wk_f).astype(BF16), vb, tn, preferred_element_type=F32) * bd
            kvb_ref[n] = lax.dot_general((kf * wk_b).astype(BF16), vb, tn, preferred_element_type=F32) * bd
            kcat = jnp.concatenate([kb * m0, kb * m1], axis=0)
            s = lax.dot_general(qb, kcat, nt, preferred_element_type=F32) * dcat
            a_ref[n] = jnp.concatenate([s.astype(BF16), qb * wq_f, qb * wq_b], axis=1)
        return carry

    lax.fori_loop(0, n_chunks // group, chunk_local, 0)

    def scan(n, carry):
        rf, rb = carry
        nb = n_chunks - 1 - n
        rf_ref[n] = rf.astype(BF16)
        rb_ref[nb] = rb.astype(BF16)
        return dc_f * rf + kvf_ref[n], dc_b * rb + kvb_ref[nb]

    zero = jnp.zeros((LANES, LANES), F32)
    lax.fori_loop(0, n_chunks, scan, (zero, zero))

    gn = gn_ref[...]

    def outputs(i, carry):
        for u in range(group):
            n = i * group + u
            r0 = pl.multiple_of(n * c, c)
            vb = v_ref[pl.ds(r0, c), :]
            bm = jnp.concatenate([vb * m0, vb * m1, rf_ref[n], rb_ref[n]], axis=0)
            o = jnp.dot(a_ref[n], bm, preferred_element_type=F32)
            o2 = o * o
            ms0 = jnp.sum(o2 * m0f, axis=-1, keepdims=True)
            ms1 = jnp.sum(o2 * (1.0 - m0f), axis=-1, keepdims=True)
            ms = jnp.where(first_l, ms0, ms1) * (1.0 / hd)
            y = o * lax.rsqrt(ms + EPS) * gn
            gate = g_ref[pl.ds(r0, c), :].astype(F32)
            y = y * (gate * jax.nn.sigmoid(gate))
            o_ref[pl.ds(r0, c), :] = y.astype(BF16)
        return carry

    lax.fori_loop(0, n_chunks // group, outputs, 0)


def _retention(dec, proj3, gn_lanes, ret_width, chunk, group=8):
    b, s, _ = proj3.shape
    npair = ret_width // LANES
    assert chunk == LANES
    blk = lambda o: pl.BlockSpec((None, s, LANES), lambda i, j, o=o: (i, 0, o * npair + j))
    return pl.pallas_call(
        functools.partial(_retention_kernel, chunk=chunk, group=group),
        grid=(b, npair),
        in_specs=[
            pl.BlockSpec(memory_space=pltpu.SMEM),
            blk(0), blk(1), blk(2), blk(3),
            pl.BlockSpec((1, LANES), lambda i, j: (0, 0)),
        ],
        out_specs=pl.BlockSpec((None, s, LANES), lambda i, j: (i, 0, j)),
        out_shape=jax.ShapeDtypeStruct((b, s, ret_width), BF16),
        scratch_shapes=(
            [pltpu.VMEM((s // chunk, LANES, LANES), F32)] * 2
            + [pltpu.VMEM((s // chunk, LANES, LANES), BF16)] * 2
            + [pltpu.VMEM((s // chunk, chunk, 4 * chunk), BF16)]
        ),
        compiler_params=_cparams(("parallel", "parallel")),
        name="retention",
    )(dec, proj3, proj3, proj3, proj3, gn_lanes)


def _post_kernel(x_ref, r_ref, a_ref, wo_ref, g_ref, wg_ref, wu_ref, wd_ref, o_ref, *, f_chunks):
    mix = jnp.concatenate([r_ref[...], a_ref[...]], axis=1)
    x1 = x_ref[...] + jnp.dot(mix, wo_ref[...], preferred_element_type=F32)
    h = (x1 * lax.rsqrt(jnp.mean(x1 * x1, axis=-1, keepdims=True) + EPS) * g_ref[...]).astype(BF16)
    acc = x1
    for f0, f1 in f_chunks:
        gate = jnp.dot(h, wg_ref[:, f0:f1], preferred_element_type=F32)
        up = jnp.dot(h, wu_ref[:, f0:f1], preferred_element_type=F32)
        act = (gate * jax.nn.sigmoid(gate) * up).astype(BF16)
        acc = acc + jnp.dot(act, wd_ref[f0:f1, :], preferred_element_type=F32)
    o_ref[...] = acc


def _f_chunks(d_ff, mxu_cols=256, max_cols=1536):
    out, f0 = [], 0
    while f0 < d_ff:
        f1 = min(d_ff, f0 + max_cols)
        out.append((f0, f1))
        f0 = f1
    assert all((a % mxu_cols == 0) for a, _ in out)
    return tuple(out)


def _post(x2, ret2, da2, wo, g, wg, wu, wd, tm):
    m, d = x2.shape
    rw, aw = ret2.shape[1], da2.shape[1]
    d_ff = wg.shape[1]
    const = lambda shape: pl.BlockSpec(shape, lambda i: (0, 0), pipeline_mode=pl.Buffered(1))
    return pl.pallas_call(
        functools.partial(_post_kernel, f_chunks=_f_chunks(d_ff)),
        grid=(m // tm,),
        in_specs=[
            pl.BlockSpec((tm, d), lambda i: (i, 0)),
            pl.BlockSpec((tm, rw), lambda i: (i, 0)),
            pl.BlockSpec((tm, aw), lambda i: (i, 0)),
            const((rw + aw, d)),
            const((1, d)),
            const((d, d_ff)),
            const((d, d_ff)),
            const((d_ff, d)),
        ],
        out_specs=pl.BlockSpec((tm, d), lambda i: (i, 0)),
        out_shape=jax.ShapeDtypeStruct((m, d), F32),
        compiler_params=_cparams(("parallel",)),
        name="post",
    )(x2, ret2, da2, wo, g, wg, wu, wd)


def _slope_pieces(n_heads):
    slopes = (2.0 ** (-8.0 * np.arange(1, n_heads + 1, dtype=np.float64) / n_heads) * LOG2E).astype(np.float32)
    cols = np.zeros((n_heads, 2, DIFF_HEAD_DIM, 1), np.float32)
    rem = slopes.astype(np.float64)
    for p in range(N_PIECES):
        piece = rem.astype(BF16).astype(np.float64)
        for dgt in range(N_DIGITS):
            cols[:, 0, N_DIGITS * p + dgt, 0] = piece
            cols[:, 1, N_AUG // 2 + N_DIGITS * p + dgt, 0] = piece
        rem = rem - piece
    return tuple(float(v) for v in slopes), jnp.asarray(cols)


def kernel(x, attn_norm_g, w_in, ret_decay_fwd, ret_decay_bwd, ret_norm_g, dq_norm_g, dk_norm_g,
           lambda_q1, lambda_k1, lambda_q2, lambda_k2, diff_norm_g, w_out, ffn_norm_g,
           w_gate, w_up, w_down):
    b, s, d = x.shape
    depth = w_in.shape[0]
    ret_width = d // 2
    diff_width = d - ret_width
    n_dheads = diff_width // DIFF_V_DIM
    tm = 512
    tq = min(1024, s)
    tk = 512
    chunk = 128
    slopes_l2, cpieces = _slope_pieces(n_dheads)

    x2 = x.reshape(b * s, d)
    for l in range(depth):
        lam_init = 0.8 - 0.6 * math.exp(-0.3 * l)
        pr, qt, ka, vt, qn2, kn2 = _in_proj(
            x2.reshape(b, s, d), attn_norm_g[l][None], w_in, l, dq_norm_g[l].astype(F32)[:, None],
            dk_norm_g[l].astype(F32)[:, None], cpieces, n_dheads, 4 * ret_width, tm)

        dec = jnp.stack([ret_decay_fwd[l], ret_decay_bwd[l]]).astype(F32)
        gn_lanes = jnp.tile(ret_norm_g[l].astype(F32), LANES // RET_HEAD_DIM)[None]
        ret = _retention(dec, pr, gn_lanes, ret_width, chunk)

        lam_params = jnp.stack([lambda_q1[l], lambda_k1[l], lambda_q2[l], lambda_k2[l]]).astype(F32)
        da = _diffattn(lam_params, qt, ka, vt, qn2, kn2, diff_norm_g[l].astype(F32)[:, None],
                       tq, tk, lam_init, slopes_l2)

        x2 = _post(x2, ret.reshape(b * s, ret_width), da.reshape(b * s, diff_width),
                   w_out[l].astype(BF16), ffn_norm_g[l][None],
                   w_gate[l].astype(BF16), w_up[l].astype(BF16), w_down[l].astype(BF16), tm)
    return x2.reshape(b, s, d)
```

```python
import functools
import math

import numpy as np
import jax
import jax.numpy as jnp
from jax import lax
from jax.experimental import pallas as pl
from jax.experimental.pallas import tpu as pltpu

EPS = 1e-6
LOG2E = 1.4426950408889634

RET_HEAD_DIM = 64
DIFF_HEAD_DIM = 64
DIFF_V_DIM = 128
LANES = 128
VMEM_LIMIT = 56 * 1024 * 1024

BF16 = jnp.bfloat16
F32 = jnp.float32


def _cparams(sem):
    return pltpu.CompilerParams(dimension_semantics=sem, vmem_limit_bytes=VMEM_LIMIT)


N_DIGITS = 3
N_PIECES = 3
N_AUG = 2 * N_DIGITS * N_PIECES
POS_RADIX = 32
CAST_COLS = 512


def _pos_digits(pos):
    d0 = pos % POS_RADIX
    d1 = (pos // POS_RADIX) % POS_RADIX * POS_RADIX
    d2 = pos // (POS_RADIX * POS_RADIX) * (POS_RADIX * POS_RADIX)
    return [d.astype(F32) for d in (d0, d1, d2)]


def _in_proj_kernel(x_ref, g_ref, w_ref, gq_ref, gk_ref, cp_ref,
                    pr_ref, qt_ref, ka_ref, vt_ref, qn2_ref, kn2_ref, wb_ref, *, n_heads, ret_cols):
    ts = x_ref.shape[0]
    ng = 2 * n_heads
    hd = DIFF_HEAD_DIM
    half = N_AUG // 2
    aw = n_heads * DIFF_V_DIM

    @pl.when((pl.program_id(0) == 0) & (pl.program_id(1) == 0))
    def _():
        for c0 in range(0, w_ref.shape[1], CAST_COLS):
            wb_ref[:, c0:c0 + CAST_COLS] = w_ref[:, c0:c0 + CAST_COLS].astype(BF16)

    x = x_ref[...]
    y = (x * lax.rsqrt(jnp.mean(x * x, axis=-1, keepdims=True) + EPS) * g_ref[...]).astype(BF16)
    dqkv = jnp.dot(y, wb_ref[:, ret_cols:], preferred_element_type=F32)
    pr_ref[...] = jnp.dot(y, wb_ref[:, :ret_cols], preferred_element_type=F32).astype(BF16)

    def head_norm_t(xf, g_col, scale):
        xt = xf.T.reshape(ng, hd, ts)
        ms = jnp.mean(xt * xt, axis=1, keepdims=True)
        return (xt * lax.rsqrt(ms + EPS) * (g_col[None] * scale)).astype(BF16)

    qn = head_norm_t(dqkv[:, :aw], gq_ref[...], DIFF_HEAD_DIM ** -0.5 * LOG2E)
    kn = head_norm_t(dqkv[:, aw:2 * aw], gk_ref[...], 1.0)
    qn2_ref[...] = jnp.sum(jnp.square(qn.astype(F32)), axis=1, keepdims=True)
    kn2_ref[...] = jnp.sum(jnp.square(kn.astype(F32)), axis=1, keepdims=True)

    pos = pl.program_id(1) * ts + lax.broadcasted_iota(jnp.int32, (1, ts), 1)
    d0, d1, d2 = _pos_digits(pos)
    r = lax.broadcasted_iota(jnp.int32, (hd, 1), 0)
    which = jnp.where(r < half, r, r - half) % N_DIGITS
    digit = jnp.where(which == 0, d0, jnp.where(which == 1, d1, d2))
    k_digits = jnp.where(r < half, digit, 0.0)
    q_digits = jnp.where((r >= half) & (r < N_AUG), -digit, 0.0)
    for g in range(ng):
        q_rows = (q_digits + cp_ref[g // 2, 0]).astype(BF16)
        k_rows = (k_digits + cp_ref[g // 2, 1]).astype(BF16)
        qt_ref[g] = jnp.concatenate([qn[g], q_rows], axis=0)
        ka_ref[g] = jnp.concatenate([kn[g], k_rows], axis=0).astype(F32).T.astype(BF16)
    vt_ref[...] = dqkv[:, 2 * aw:].T.reshape(n_heads, DIFF_V_DIM, ts).astype(BF16)


def _in_proj(x3, g, w_all, layer, gq, gk, cpieces, n_heads, ret_cols, ts):
    b, s, d = x3.shape
    n = w_all.shape[2]
    ng = 2 * n_heads
    assert n == ret_cols + 3 * n_heads * DIFF_V_DIM and n % CAST_COLS == 0
    const = lambda shape: pl.BlockSpec(shape, lambda i, j: (0,) * len(shape), pipeline_mode=pl.Buffered(1))
    return pl.pallas_call(
        functools.partial(_in_proj_kernel, n_heads=n_heads, ret_cols=ret_cols),
        grid=(b, s // ts),
        in_specs=[
            pl.BlockSpec((None, ts, d), lambda i, j: (i, j, 0)),
            const((1, d)),
            pl.BlockSpec((None, d, n), lambda i, j: (layer, 0, 0), pipeline_mode=pl.Buffered(1)),
            const((DIFF_HEAD_DIM, 1)),
            const((DIFF_HEAD_DIM, 1)),
            const((n_heads, 2, DIFF_HEAD_DIM, 1)),
        ],
        out_specs=[
            pl.BlockSpec((None, ts, ret_cols), lambda i, j: (i, j, 0)),
            pl.BlockSpec((None, ng, LANES, ts), lambda i, j: (i, 0, 0, j)),
            pl.BlockSpec((None, ng, ts, LANES), lambda i, j: (i, 0, j, 0)),
            pl.BlockSpec((None, n_heads, DIFF_V_DIM, ts), lambda i, j: (i, 0, 0, j)),
            pl.BlockSpec((None, ng, 1, ts), lambda i, j: (i, 0, 0, j)),
            pl.BlockSpec((None, ng, 1, ts), lambda i, j: (i, 0, 0, j)),
        ],
        out_shape=[
            jax.ShapeDtypeStruct((b, s, ret_cols), BF16),
            jax.ShapeDtypeStruct((b, ng, LANES, s), BF16),
            jax.ShapeDtypeStruct((b, ng, s, LANES), BF16),
            jax.ShapeDtypeStruct((b, n_heads, DIFF_V_DIM, s), BF16),
            jax.ShapeDtypeStruct((b, ng, 1, s), F32),
            jax.ShapeDtypeStruct((b, ng, 1, s), F32),
        ],
        scratch_shapes=[pltpu.VMEM((d, n), BF16)],
        compiler_params=_cparams(("arbitrary", "arbitrary")),
        name="in_proj",
    )(x3, g, w_all, gq, gk, cpieces)


NEG_BIG = -1e30
SAFE_LOG2_RANGE = 96.0


def _diffattn_kernel(lam_ref, qt_ref, ka_ref, vt_ref, qn2_ref, kn2_ref, gd_ref, o_ref,
                     qs_ref, corr_ref, m_ref, l_ref, acc_ref, s_refs, p_refs, bmax_refs, kmax_ref,
                     *, tq, tk, lam_init, slopes_l2):
    h = pl.program_id(1)
    qi = pl.program_id(2)
    s_len = ka_ref.shape[1]
    nk = s_len // tk
    ratio = tq // tk
    assert tq % tk == 0 and nk % 2 == 0
    slope = jnp.float32(0.0)
    for hh, sv in enumerate(slopes_l2):
        slope = jnp.where(h == hh, jnp.float32(sv), slope)

    @pl.when(qi == 0)
    def _():
        ii = lax.broadcasted_iota(jnp.int32, (tk, tq), 0)
        jj = lax.broadcasted_iota(jnp.int32, (tk, tq), 1)
        corr_ref[0] = jnp.zeros((tk, tq), F32)
        for d in range(ratio):
            corr_ref[1 + d] = (2.0 * slope) * jnp.maximum(ii - jj + d * tk, 0).astype(F32)
        for mp in range(2):
            kmax_ref[mp] = jnp.max(kn2_ref[mp])

    row = lax.broadcasted_iota(jnp.int32, (LANES, 1), 0)
    sign = jnp.where(row < DIFF_HEAD_DIM, 1.0, -1.0).astype(BF16)
    for mp in range(2):
        q = qt_ref[mp]
        qs_ref[0, mp] = q
        qs_ref[1, mp] = q * sign

    def block_start(j):
        return j * tk if isinstance(j, int) else pl.multiple_of(j * tk, tk)

    def raw_scores(j, mp):
        d = j - ratio * qi
        above = (d >= ratio).astype(jnp.int32)
        overlap = jnp.where((d >= 0) & (d < ratio), d + 1, 0)
        k = ka_ref[mp, pl.ds(block_start(j), tk), :]
        return jnp.dot(k, qs_ref[above, mp], preferred_element_type=F32) - corr_ref[overlap]

    def weighted_values(j, slot):
        return jnp.dot(vt_ref[:, pl.ds(block_start(j), tk)], p_refs[slot][...], preferred_element_type=F32)

    def add_values(pv, first=False):
        for mp in range(2):
            part = pv[:, mp * tq:(mp + 1) * tq]
            acc_ref[mp] = part if first else acc_ref[mp] + part

    def bounded_path():
        for j in range(nk):
            slot = j % 2
            if j > 0:
                pv = weighted_values(j - 1, 1 - slot)
            for mp in range(2):
                p = jnp.exp2(raw_scores(j, mp))
                psum = jnp.sum(p, axis=0, keepdims=True)
                l_ref[mp] = psum if j == 0 else l_ref[mp] + psum
                p_refs[slot][:, mp * tq:(mp + 1) * tq] = p.astype(BF16)
            if j > 0:
                add_values(pv, first=j == 1)
        add_values(weighted_values(nk - 1, (nk - 1) % 2))

    def online_max_path():
        m_ref[...] = jnp.full(m_ref.shape, NEG_BIG, F32)
        l_ref[...] = jnp.zeros(l_ref.shape, F32)
        acc_ref[...] = jnp.zeros(acc_ref.shape, F32)
        p_refs[1][...] = jnp.zeros(p_refs[1].shape, BF16)

        def scores(j, slot):
            for mp in range(2):
                s = raw_scores(j, mp)
                s_refs[slot][mp] = s
                bmax_refs[slot][mp] = jnp.max(s, axis=0, keepdims=True)

        def region(j, slot):
            pv = weighted_values(jnp.maximum(j - 1, 0), 1 - slot)
            alphas = []
            for mp in range(2):
                m_old = m_ref[mp]
                m_new = jnp.maximum(m_old, bmax_refs[slot][mp])
                p = jnp.exp2(s_refs[slot][mp] - m_new)
                alpha = jnp.exp2(m_old - m_new)
                l_ref[mp] = alpha * l_ref[mp] + jnp.sum(p, axis=0, keepdims=True)
                m_ref[mp] = m_new
                p_refs[slot][:, mp * tq:(mp + 1) * tq] = p.astype(BF16)
                alphas.append(alpha)
            scores(jnp.minimum(j + 1, nk - 1), 1 - slot)
            for mp in range(2):
                acc_ref[mp] = (acc_ref[mp] + pv[:, mp * tq:(mp + 1) * tq]) * alphas[mp]

        scores(0, 0)

        def pair(i, c):
            region(2 * i, 0)
            region(2 * i + 1, 1)
            return c

        lax.fori_loop(0, nk // 2, pair, 0)
        add_values(weighted_values(nk - 1, 1))

    bound2 = jnp.float32(0.0)
    for mp in range(2):
        bound2 = jnp.maximum(bound2, jnp.max(qn2_ref[mp]) * kmax_ref[mp])
    lax.cond(bound2 <= SAFE_LOG2_RANGE ** 2, bounded_path, online_max_path)

    la = lam_ref[...]
    lam = (jnp.exp(jnp.sum(la[0:1] * la[1:2], axis=-1, keepdims=True))
           - jnp.exp(jnp.sum(la[2:3] * la[3:4], axis=-1, keepdims=True)) + lam_init)
    o1 = acc_ref[0] * (1.0 / l_ref[0])
    o2 = acc_ref[1] * (lam / l_ref[1])
    da = o1 - o2
    y = da * lax.rsqrt(jnp.mean(da * da, axis=0, keepdims=True) + EPS)
    y = y * (gd_ref[...] * (1.0 - lam_init))
    o_ref[...] = y.T.astype(BF16)


def _diffattn(lam_params, qt, ka, vt, qn2, kn2, gd_col, tq, tk, lam_init, slopes_l2):
    b, ng, _, s = qt.shape
    n_heads = ng // 2
    kern = functools.partial(_diffattn_kernel, tq=tq, tk=tk, lam_init=lam_init, slopes_l2=slopes_l2)
    return pl.pallas_call(
        kern,
        grid=(b, n_heads, s // tq),
        in_specs=[
            pl.BlockSpec((4, DIFF_HEAD_DIM), lambda i, h, q: (0, 0)),
            pl.BlockSpec((None, 2, LANES, tq), lambda i, h, q: (i, h, 0, q)),
            pl.BlockSpec((None, 2, s, LANES), lambda i, h, q: (i, h, 0, 0)),
            pl.BlockSpec((None, None, DIFF_V_DIM, s), lambda i, h, q: (i, h, 0, 0)),
            pl.BlockSpec((None, 2, 1, tq), lambda i, h, q: (i, h, 0, q)),
            pl.BlockSpec((None, 2, 1, s), lambda i, h, q: (i, h, 0, 0)),
            pl.BlockSpec((DIFF_V_DIM, 1), lambda i, h, q: (0, 0)),
        ],
        out_specs=pl.BlockSpec((None, tq, DIFF_V_DIM), lambda i, h, q: (i, q, h)),
        out_shape=jax.ShapeDtypeStruct((b, s, n_heads * DIFF_V_DIM), BF16),
        scratch_shapes=[
            pltpu.VMEM((2, 2, LANES, tq), BF16),
            pltpu.VMEM((1 + tq // tk, tk, tq), F32),
            pltpu.VMEM((2, 1, tq), F32),
            pltpu.VMEM((2, 1, tq), F32),
            pltpu.VMEM((2, DIFF_V_DIM, tq), F32),
            [pltpu.VMEM((2, tk, tq), F32)] * 2,
            [pltpu.VMEM((tk, 2 * tq), BF16)] * 2,
            [pltpu.VMEM((2, 1, tq), F32)] * 2,
            pltpu.SMEM((2,), F32),
        ],
        compiler_params=_cparams(("parallel", "parallel", "arbitrary")),
        name="diffattn",
    )(lam_params, qt, ka, vt, qn2, kn2, gd_col)


def _retention_kernel(dec_ref, q_ref, k_ref, v_ref, g_ref, gn_ref, o_ref,
                      kvf_ref, kvb_ref, rf_ref, rb_ref, a_ref, *, chunk, group):
    j = pl.program_id(1)
    s_len = q_ref.shape[0]
    n_chunks = s_len // chunk
    hd = RET_HEAD_DIM
    c = chunk

    lane = lax.broadcasted_iota(jnp.int32, (1, LANES), 1)
    rowi = lax.broadcasted_iota(jnp.int32, (LANES, 1), 0)
    first_l = lane < hd
    first_r = rowi < hd

    def per_lane(d):
        return -jnp.exp(jnp.where(first_l, dec_ref[d, 2 * j], dec_ref[d, 2 * j + 1]))

    def per_row(d):
        return -jnp.exp(jnp.where(first_r, dec_ref[d, 2 * j], dec_ref[d, 2 * j + 1]))

    lgf_l, lgb_l = per_lane(0), per_lane(1)
    lgf_r, lgb_r = per_row(0), per_row(1)

    pos_r = lax.broadcasted_iota(jnp.int32, (c, 1), 0).astype(F32)
    wk_f = jnp.exp(lgf_l * (c - 1.0 - pos_r))
    wq_f = jnp.exp(lgf_l * (pos_r + 1.0))
    wk_b = jnp.exp(lgb_l * pos_r)
    wq_b = jnp.exp(lgb_l * (c - pos_r))
    dc_f = jnp.exp(lgf_r * float(c))
    dc_b = jnp.exp(lgb_r * float(c))

    tt = lax.broadcasted_iota(jnp.int32, (c, c), 0)
    ss = lax.broadcasted_iota(jnp.int32, (c, c), 1)
    dist = (tt - ss).astype(F32)

    def dmask(hh):
        lf = -jnp.exp(jnp.full((1, 1), dec_ref[0, 2 * j + hh], F32))
        lb = -jnp.exp(jnp.full((1, 1), dec_ref[1, 2 * j + hh], F32))
        return jnp.where(dist >= 0, jnp.exp(lf * jnp.maximum(dist, 0.0)),
                         jnp.exp(lb * jnp.maximum(-dist, 0.0)))

    kscale = RET_HEAD_DIM ** -0.5
    dcat = jnp.concatenate([dmask(0), dmask(1)], axis=1) * kscale
    wk_f = wk_f * kscale
    wk_b = wk_b * kscale
    wq_f = wq_f.astype(BF16)
    wq_b = wq_b.astype(BF16)
    bd = (first_r == first_l).astype(F32).astype(BF16)
    m0f = first_l.astype(F32)
    m0 = m0f.astype(BF16)
    m1 = (1.0 - m0f).astype(BF16)

    tn = (((0,), (0,)), ((), ()))
    nt = (((1,), (1,)), ((), ()))
    assert n_chunks % group == 0

    def chunk_local(i, carry):
        for u in range(group):
            n = i * group + u
            r0 = pl.multiple_of(n * c, c)
            qb = q_ref[pl.ds(r0, c), :]
            kb = k_ref[pl.ds(r0, c), :]
            vb = v_ref[pl.ds(r0, c), :]
            kf = kb.astype(F32)
            kvf_ref[n] = lax.dot_general((kf * wk_f).astype(BF16), vb, tn, preferred_element_type=F32)
            kvb_ref[n] = lax.dot_general((kf * wk_b).astype(BF16), vb, tn, preferred_element_type=F32)
            kcat = jnp.concatenate([kb * m0, kb * m1], axis=0)
            s = lax.dot_general(qb, kcat, nt, preferred_element_type=F32) * dcat
            a_ref[n] = jnp.concatenate([s.astype(BF16), qb * wq_f, qb * wq_b], axis=1)
        return carry

    lax.fori_loop(0, n_chunks // group, chunk_local, 0)

    def scan(n, carry):
        rf, rb = carry
        nb = n_chunks - 1 - n
        rf_ref[n] = rf.astype(BF16)
        rb_ref[nb] = rb.astype(BF16)
        return dc_f * rf + kvf_ref[n], dc_b * rb + kvb_ref[nb]

    zero = jnp.zeros((LANES, LANES), F32)
    lax.fori_loop(0, n_chunks, scan, (zero, zero))

    gn = gn_ref[...]

    def outputs(i, carry):
        for u in range(group):
            n = i * group + u
            r0 = pl.multiple_of(n * c, c)
            vb = v_ref[pl.ds(r0, c), :]
            bm = jnp.concatenate([vb * m0, vb * m1, rf_ref[n] * bd, rb_ref[n] * bd], axis=0)
            o = jnp.dot(a_ref[n], bm, preferred_element_type=F32)
            o2 = o * o
            ms0 = jnp.sum(o2 * m0f, axis=-1, keepdims=True)
            ms1 = jnp.sum(o2 * (1.0 - m0f), axis=-1, keepdims=True)
            ms = jnp.where(first_l, ms0, ms1) * (1.0 / hd)
            y = o * lax.rsqrt(ms + EPS) * gn
            gate = g_ref[pl.ds(r0, c), :].astype(F32)
            y = y * (gate * jax.nn.sigmoid(gate))
            o_ref[pl.ds(r0, c), :] = y.astype(BF16)
        return carry

    lax.fori_loop(0, n_chunks // group, outputs, 0)


def _retention(dec, proj3, gn_lanes, ret_width, chunk, group=8):
    b, s, _ = proj3.shape
    npair = ret_width // LANES
    assert chunk == LANES
    blk = lambda o: pl.BlockSpec((None, s, LANES), lambda i, j, o=o: (i, 0, o * npair + j))
    return pl.pallas_call(
        functools.partial(_retention_kernel, chunk=chunk, group=group),
        grid=(b, npair),
        in_specs=[
            pl.BlockSpec(memory_space=pltpu.SMEM),
            blk(0), blk(1), blk(2), blk(3),
            pl.BlockSpec((1, LANES), lambda i, j: (0, 0)),
        ],
        out_specs=pl.BlockSpec((None, s, LANES), lambda i, j: (i, 0, j)),
        out_shape=jax.ShapeDtypeStruct((b, s, ret_width), BF16),
        scratch_shapes=(
            [pltpu.VMEM((s // chunk, LANES, LANES), F32)] * 2
            + [pltpu.VMEM((s // chunk, LANES, LANES), BF16)] * 2
            + [pltpu.VMEM((s // chunk, chunk, 4 * chunk), BF16)]
        ),
        compiler_params=_cparams(("parallel", "parallel")),
        name="retention",
    )(dec, proj3, proj3, proj3, proj3, gn_lanes)


def _post_kernel(x_ref, r_ref, a_ref, wo_ref, g_ref, wg_ref, wu_ref, wd_ref, o_ref, *, f_chunks):
    mix = jnp.concatenate([r_ref[...], a_ref[...]], axis=1)
    x1 = x_ref[...] + jnp.dot(mix, wo_ref[...], preferred_element_type=F32)
    h = (x1 * lax.rsqrt(jnp.mean(x1 * x1, axis=-1, keepdims=True) + EPS) * g_ref[...]).astype(BF16)
    acc = x1
    for f0, f1 in f_chunks:
        gate = jnp.dot(h, wg_ref[:, f0:f1], preferred_element_type=F32)
        up = jnp.dot(h, wu_ref[:, f0:f1], preferred_element_type=F32)
        act = (gate * jax.nn.sigmoid(gate) * up).astype(BF16)
        acc = acc + jnp.dot(act, wd_ref[f0:f1, :], preferred_element_type=F32)
    o_ref[...] = acc


def _f_chunks(d_ff, mxu_cols=256, max_cols=1536):
    out, f0 = [], 0
    while f0 < d_ff:
        f1 = min(d_ff, f0 + max_cols)
        out.append((f0, f1))
        f0 = f1
    assert all((a % mxu_cols == 0) for a, _ in out)
    return tuple(out)


def _post(x2, ret2, da2, wo, g, wg, wu, wd, layer, tm):
    m, d = x2.shape
    rw, aw = ret2.shape[1], da2.shape[1]
    d_ff = wg.shape[2]
    weight = lambda rows, cols: pl.BlockSpec((None, rows, cols), lambda i: (layer, 0, 0),
                                             pipeline_mode=pl.Buffered(1))
    return pl.pallas_call(
        functools.partial(_post_kernel, f_chunks=_f_chunks(d_ff)),
        grid=(m // tm,),
        in_specs=[
            pl.BlockSpec((tm, d), lambda i: (i, 0)),
            pl.BlockSpec((tm, rw), lambda i: (i, 0)),
            pl.BlockSpec((tm, aw), lambda i: (i, 0)),
            weight(rw + aw, d),
            pl.BlockSpec((1, d), lambda i: (0, 0), pipeline_mode=pl.Buffered(1)),
            weight(d, d_ff),
            weight(d, d_ff),
            weight(d_ff, d),
        ],
        out_specs=pl.BlockSpec((tm, d), lambda i: (i, 0)),
        out_shape=jax.ShapeDtypeStruct((m, d), F32),
        compiler_params=_cparams(("parallel",)),
        name="post",
    )(x2, ret2, da2, wo, g, wg, wu, wd)


def _slope_pieces(n_heads):
    slopes = (2.0 ** (-8.0 * np.arange(1, n_heads + 1, dtype=np.float64) / n_heads) * LOG2E).astype(np.float32)
    cols = np.zeros((n_heads, 2, DIFF_HEAD_DIM, 1), np.float32)
    rem = slopes.astype(np.float64)
    for p in range(N_PIECES):
        piece = rem.astype(BF16).astype(np.float64)
        for dgt in range(N_DIGITS):
            cols[:, 0, N_DIGITS * p + dgt, 0] = piece
            cols[:, 1, N_AUG // 2 + N_DIGITS * p + dgt, 0] = piece
        rem = rem - piece
    return tuple(float(v) for v in slopes), jnp.asarray(cols)


def kernel(x, attn_norm_g, w_in, ret_decay_fwd, ret_decay_bwd, ret_norm_g, dq_norm_g, dk_norm_g,
           lambda_q1, lambda_k1, lambda_q2, lambda_k2, diff_norm_g, w_out, ffn_norm_g,
           w_gate, w_up, w_down):
    b, s, d = x.shape
    depth = w_in.shape[0]
    ret_width = d // 2
    diff_width = d - ret_width
    n_dheads = diff_width // DIFF_V_DIM
    tm = 512
    tq = min(1024, s)
    tk = 512
    chunk = 128
    slopes_l2, cpieces = _slope_pieces(n_dheads)

    wo_b, wg_b, wu_b, wd_b = (w.astype(BF16) for w in (w_out, w_gate, w_up, w_down))

    x2 = x.reshape(b * s, d)
    for l in range(depth):
        lam_init = 0.8 - 0.6 * math.exp(-0.3 * l)
        pr, qt, ka, vt, qn2, kn2 = _in_proj(
            x2.reshape(b, s, d), attn_norm_g[l][None], w_in, l, dq_norm_g[l].astype(F32)[:, None],
            dk_norm_g[l].astype(F32)[:, None], cpieces, n_dheads, 4 * ret_width, tm)

        dec = jnp.stack([ret_decay_fwd[l], ret_decay_bwd[l]]).astype(F32)
        gn_lanes = jnp.tile(ret_norm_g[l].astype(F32), LANES // RET_HEAD_DIM)[None]
        ret = _retention(dec, pr, gn_lanes, ret_width, chunk)

        lam_params = jnp.stack([lambda_q1[l], lambda_k1[l], lambda_q2[l], lambda_k2[l]]).astype(F32)
        da = _diffattn(lam_params, qt, ka, vt, qn2, kn2, diff_norm_g[l].astype(F32)[:, None],
                       tq, tk, lam_init, slopes_l2)

        x2 = _post(x2, ret.reshape(b * s, ret_width), da.reshape(b * s, diff_width),
                   wo_b, ffn_norm_g[l][None], wg_b, wu_b, wd_b, l, tm)
    return x2.reshape(b, s, d)
```

```python
import functools
import math

import numpy as np
import jax
import jax.numpy as jnp
from jax import lax
from jax.experimental import pallas as pl
from jax.experimental.pallas import tpu as pltpu

EPS = 1e-6
LOG2E = 1.4426950408889634

RET_HEAD_DIM = 64
DIFF_HEAD_DIM = 64
DIFF_V_DIM = 128
LANES = 128
VMEM_LIMIT = 56 * 1024 * 1024

BF16 = jnp.bfloat16
F32 = jnp.float32


def _cparams(sem):
    return pltpu.CompilerParams(dimension_semantics=sem, vmem_limit_bytes=VMEM_LIMIT)


N_DIGITS = 3
N_PIECES = 3
N_AUG = 2 * N_DIGITS * N_PIECES
POS_RADIX = 32
CAST_COLS = 512


def _pos_digits(pos):
    d0 = pos % POS_RADIX
    d1 = (pos // POS_RADIX) % POS_RADIX * POS_RADIX
    d2 = pos // (POS_RADIX * POS_RADIX) * (POS_RADIX * POS_RADIX)
    return [d.astype(F32) for d in (d0, d1, d2)]


def _in_proj_kernel(x_ref, g_ref, w_ref, gq_ref, gk_ref, cp_ref,
                    pr_ref, qt_ref, ka_ref, vt_ref, qn2_ref, kn2_ref, wb_ref, *, n_heads, ret_cols):
    ts = x_ref.shape[0]
    ng = 2 * n_heads
    hd = DIFF_HEAD_DIM
    half = N_AUG // 2
    aw = n_heads * DIFF_V_DIM

    @pl.when((pl.program_id(0) == 0) & (pl.program_id(1) == 0))
    def _():
        for c0 in range(0, w_ref.shape[1], CAST_COLS):
            wb_ref[:, c0:c0 + CAST_COLS] = w_ref[:, c0:c0 + CAST_COLS].astype(BF16)

    x = x_ref[...]
    y = (x * lax.rsqrt(jnp.mean(x * x, axis=-1, keepdims=True) + EPS) * g_ref[...]).astype(BF16)
    dqkv = jnp.dot(y, wb_ref[:, ret_cols:], preferred_element_type=F32)
    pr_ref[...] = jnp.dot(y, wb_ref[:, :ret_cols], preferred_element_type=F32).astype(BF16)

    def head_norm_t(xf, g_col, scale):
        xt = xf.T.reshape(ng, hd, ts)
        ms = jnp.mean(xt * xt, axis=1, keepdims=True)
        return (xt * lax.rsqrt(ms + EPS) * (g_col[None] * scale)).astype(BF16)

    qn = head_norm_t(dqkv[:, :aw], gq_ref[...], DIFF_HEAD_DIM ** -0.5 * LOG2E)
    kn = head_norm_t(dqkv[:, aw:2 * aw], gk_ref[...], 1.0)
    qn2_ref[...] = jnp.sum(jnp.square(qn.astype(F32)), axis=1, keepdims=True)
    kn2_ref[...] = jnp.sum(jnp.square(kn.astype(F32)), axis=1, keepdims=True)

    pos = pl.program_id(1) * ts + lax.broadcasted_iota(jnp.int32, (1, ts), 1)
    d0, d1, d2 = _pos_digits(pos)
    r = lax.broadcasted_iota(jnp.int32, (hd, 1), 0)
    which = jnp.where(r < half, r, r - half) % N_DIGITS
    digit = jnp.where(which == 0, d0, jnp.where(which == 1, d1, d2))
    k_digits = jnp.where(r < half, digit, 0.0)
    q_digits = jnp.where((r >= half) & (r < N_AUG), -digit, 0.0)
    for g in range(ng):
        q_rows = (q_digits + cp_ref[g // 2, 0]).astype(BF16)
        k_rows = (k_digits + cp_ref[g // 2, 1]).astype(BF16)
        qt_ref[g] = jnp.concatenate([qn[g], q_rows], axis=0)
        ka_ref[g] = jnp.concatenate([kn[g], k_rows], axis=0).astype(F32).T.astype(BF16)
    vt_ref[...] = dqkv[:, 2 * aw:].T.reshape(n_heads, DIFF_V_DIM, ts).astype(BF16)


def _in_proj(x3, g, w_all, layer, gq, gk, cpieces, n_heads, ret_cols, ts):
    b, s, d = x3.shape
    n = w_all.shape[2]
    ng = 2 * n_heads
    assert n == ret_cols + 3 * n_heads * DIFF_V_DIM and n % CAST_COLS == 0
    const = lambda shape: pl.BlockSpec(shape, lambda i, j: (0,) * len(shape), pipeline_mode=pl.Buffered(1))
    return pl.pallas_call(
        functools.partial(_in_proj_kernel, n_heads=n_heads, ret_cols=ret_cols),
        grid=(b, s // ts),
        in_specs=[
            pl.BlockSpec((None, ts, d), lambda i, j: (i, j, 0)),
            const((1, d)),
            pl.BlockSpec((None, d, n), lambda i, j: (layer, 0, 0), pipeline_mode=pl.Buffered(1)),
            const((DIFF_HEAD_DIM, 1)),
            const((DIFF_HEAD_DIM, 1)),
            const((n_heads, 2, DIFF_HEAD_DIM, 1)),
        ],
        out_specs=[
            pl.BlockSpec((None, ts, ret_cols), lambda i, j: (i, j, 0)),
            pl.BlockSpec((None, ng, LANES, ts), lambda i, j: (i, 0, 0, j)),
            pl.BlockSpec((None, ng, ts, LANES), lambda i, j: (i, 0, j, 0)),
            pl.BlockSpec((None, n_heads, DIFF_V_DIM, ts), lambda i, j: (i, 0, 0, j)),
            pl.BlockSpec((None, ng, 1, ts), lambda i, j: (i, 0, 0, j)),
            pl.BlockSpec((None, ng, 1, ts), lambda i, j: (i, 0, 0, j)),
        ],
        out_shape=[
            jax.ShapeDtypeStruct((b, s, ret_cols), BF16),
            jax.ShapeDtypeStruct((b, ng, LANES, s), BF16),
            jax.ShapeDtypeStruct((b, ng, s, LANES), BF16),
            jax.ShapeDtypeStruct((b, n_heads, DIFF_V_DIM, s), BF16),
            jax.ShapeDtypeStruct((b, ng, 1, s), F32),
            jax.ShapeDtypeStruct((b, ng, 1, s), F32),
        ],
        scratch_shapes=[pltpu.VMEM((d, n), BF16)],
        compiler_params=_cparams(("arbitrary", "arbitrary")),
        name="in_proj",
    )(x3, g, w_all, gq, gk, cpieces)


NEG_BIG = -1e30
SAFE_LOG2_RANGE = 96.0


def _diffattn_kernel(lam_ref, qt_ref, ka_ref, vt_ref, qn2_ref, kn2_ref, gd_ref, o_ref,
                     qs_ref, corr_ref, m_ref, l_ref, acc_ref, s_refs, p_refs, bmax_refs, kmax_ref,
                     *, tq, tk, lam_init, slopes_l2):
    h = pl.program_id(1)
    qi = pl.program_id(2)
    s_len = ka_ref.shape[1]
    nk = s_len // tk
    ratio = tq // tk
    assert tq % tk == 0 and nk % 2 == 0
    slope = jnp.float32(0.0)
    for hh, sv in enumerate(slopes_l2):
        slope = jnp.where(h == hh, jnp.float32(sv), slope)

    @pl.when(qi == 0)
    def _():
        ii = lax.broadcasted_iota(jnp.int32, (tk, tq), 0)
        jj = lax.broadcasted_iota(jnp.int32, (tk, tq), 1)
        corr_ref[0] = jnp.zeros((tk, tq), F32)
        for d in range(ratio):
            corr_ref[1 + d] = (2.0 * slope) * jnp.maximum(ii - jj + d * tk, 0).astype(F32)
        for mp in range(2):
            kmax_ref[mp] = jnp.max(kn2_ref[mp])

    row = lax.broadcasted_iota(jnp.int32, (LANES, 1), 0)
    sign = jnp.where(row < DIFF_HEAD_DIM, 1.0, -1.0).astype(BF16)
    for mp in range(2):
        q = qt_ref[mp]
        qs_ref[0, mp] = q
        qs_ref[1, mp] = q * sign

    def block_start(j):
        return j * tk if isinstance(j, int) else pl.multiple_of(j * tk, tk)

    def raw_scores(j, mp):
        d = j - ratio * qi
        above = (d >= ratio).astype(jnp.int32)
        overlap = jnp.where((d >= 0) & (d < ratio), d + 1, 0)
        k = ka_ref[mp, pl.ds(block_start(j), tk), :]
        return jnp.dot(k, qs_ref[above, mp], preferred_element_type=F32) - corr_ref[overlap]

    def weighted_values(j, slot):
        return jnp.dot(vt_ref[:, pl.ds(block_start(j), tk)], p_refs[slot][...], preferred_element_type=F32)

    def add_values(pv, first=False):
        for mp in range(2):
            part = pv[:, mp * tq:(mp + 1) * tq]
            acc_ref[mp] = part if first else acc_ref[mp] + part

    def bounded_path():
        first_block = ratio * qi

        def block_of(r):
            j = first_block + r
            return jnp.where(j >= nk, j - nk, j)

        for r in range(nk):
            slot = r % 2
            j = block_of(r)
            if r > 0:
                pv = weighted_values(block_of(r - 1), 1 - slot)
            above = 0 if r < ratio else (first_block + r < nk).astype(jnp.int32)
            for mp in range(2):
                k = ka_ref[mp, pl.ds(block_start(j), tk), :]
                s = jnp.dot(k, qs_ref[above, mp], preferred_element_type=F32)
                if r < ratio:
                    s = s - corr_ref[1 + r]
                p = jnp.exp2(s)
                psum = jnp.sum(p, axis=0, keepdims=True)
                l_ref[mp] = psum if r == 0 else l_ref[mp] + psum
                p_refs[slot][:, mp * tq:(mp + 1) * tq] = p.astype(BF16)
            if r > 0:
                add_values(pv, first=r == 1)
        add_values(weighted_values(block_of(nk - 1), (nk - 1) % 2))

    def online_max_path():
        m_ref[...] = jnp.full(m_ref.shape, NEG_BIG, F32)
        l_ref[...] = jnp.zeros(l_ref.shape, F32)
        acc_ref[...] = jnp.zeros(acc_ref.shape, F32)
        p_refs[1][...] = jnp.zeros(p_refs[1].shape, BF16)

        def scores(j, slot):
            for mp in range(2):
                s = raw_scores(j, mp)
                s_refs[slot][mp] = s
                bmax_refs[slot][mp] = jnp.max(s, axis=0, keepdims=True)

        def region(j, slot):
            pv = weighted_values(jnp.maximum(j - 1, 0), 1 - slot)
            alphas = []
            for mp in range(2):
                m_old = m_ref[mp]
                m_new = jnp.maximum(m_old, bmax_refs[slot][mp])
                p = jnp.exp2(s_refs[slot][mp] - m_new)
                alpha = jnp.exp2(m_old - m_new)
                l_ref[mp] = alpha * l_ref[mp] + jnp.sum(p, axis=0, keepdims=True)
                m_ref[mp] = m_new
                p_refs[slot][:, mp * tq:(mp + 1) * tq] = p.astype(BF16)
                alphas.append(alpha)
            scores(jnp.minimum(j + 1, nk - 1), 1 - slot)
            for mp in range(2):
                acc_ref[mp] = (acc_ref[mp] + pv[:, mp * tq:(mp + 1) * tq]) * alphas[mp]

        scores(0, 0)

        def pair(i, c):
            region(2 * i, 0)
            region(2 * i + 1, 1)
            return c

        lax.fori_loop(0, nk // 2, pair, 0)
        add_values(weighted_values(nk - 1, 1))

    bound2 = jnp.float32(0.0)
    for mp in range(2):
        bound2 = jnp.maximum(bound2, jnp.max(qn2_ref[mp]) * kmax_ref[mp])
    lax.cond(bound2 <= SAFE_LOG2_RANGE ** 2, bounded_path, online_max_path)

    la = lam_ref[...]
    lam = (jnp.exp(jnp.sum(la[0:1] * la[1:2], axis=-1, keepdims=True))
           - jnp.exp(jnp.sum(la[2:3] * la[3:4], axis=-1, keepdims=True)) + lam_init)
    o1 = acc_ref[0] * (1.0 / l_ref[0])
    o2 = acc_ref[1] * (lam / l_ref[1])
    da = o1 - o2
    y = da * lax.rsqrt(jnp.mean(da * da, axis=0, keepdims=True) + EPS)
    y = y * (gd_ref[...] * (1.0 - lam_init))
    o_ref[...] = y.T.astype(BF16)


def _diffattn(lam_params, qt, ka, vt, qn2, kn2, gd_col, tq, tk, lam_init, slopes_l2):
    b, ng, _, s = qt.shape
    n_heads = ng // 2
    kern = functools.partial(_diffattn_kernel, tq=tq, tk=tk, lam_init=lam_init, slopes_l2=slopes_l2)
    return pl.pallas_call(
        kern,
        grid=(b, n_heads, s // tq),
        in_specs=[
            pl.BlockSpec((4, DIFF_HEAD_DIM), lambda i, h, q: (0, 0)),
            pl.BlockSpec((None, 2, LANES, tq), lambda i, h, q: (i, h, 0, q)),
            pl.BlockSpec((None, 2, s, LANES), lambda i, h, q: (i, h, 0, 0)),
            pl.BlockSpec((None, None, DIFF_V_DIM, s), lambda i, h, q: (i, h, 0, 0)),
            pl.BlockSpec((None, 2, 1, tq), lambda i, h, q: (i, h, 0, q)),
            pl.BlockSpec((None, 2, 1, s), lambda i, h, q: (i, h, 0, 0)),
            pl.BlockSpec((DIFF_V_DIM, 1), lambda i, h, q: (0, 0)),
        ],
        out_specs=pl.BlockSpec((None, tq, DIFF_V_DIM), lambda i, h, q: (i, q, h)),
        out_shape=jax.ShapeDtypeStruct((b, s, n_heads * DIFF_V_DIM), BF16),
        scratch_shapes=[
            pltpu.VMEM((2, 2, LANES, tq), BF16),
            pltpu.VMEM((1 + tq // tk, tk, tq), F32),
            pltpu.VMEM((2, 1, tq), F32),
            pltpu.VMEM((2, 1, tq), F32),
            pltpu.VMEM((2, DIFF_V_DIM, tq), F32),
            [pltpu.VMEM((2, tk, tq), F32)] * 2,
            [pltpu.VMEM((tk, 2 * tq), BF16)] * 2,
            [pltpu.VMEM((2, 1, tq), F32)] * 2,
            pltpu.SMEM((2,), F32),
        ],
        compiler_params=_cparams(("parallel", "parallel", "arbitrary")),
        name="diffattn",
    )(lam_params, qt, ka, vt, qn2, kn2, gd_col)


def _retention_kernel(dec_ref, q_ref, k_ref, v_ref, g_ref, gn_ref, o_ref,
                      kvf_ref, kvb_ref, rf_ref, rb_ref, a_ref, *, chunk, group):
    j = pl.program_id(1)
    s_len = q_ref.shape[0]
    n_chunks = s_len // chunk
    hd = RET_HEAD_DIM
    c = chunk

    lane = lax.broadcasted_iota(jnp.int32, (1, LANES), 1)
    rowi = lax.broadcasted_iota(jnp.int32, (LANES, 1), 0)
    first_l = lane < hd
    first_r = rowi < hd

    def per_lane(d):
        return -jnp.exp(jnp.where(first_l, dec_ref[d, 2 * j], dec_ref[d, 2 * j + 1]))

    def per_row(d):
        return -jnp.exp(jnp.where(first_r, dec_ref[d, 2 * j], dec_ref[d, 2 * j + 1]))

    lgf_l, lgb_l = per_lane(0), per_lane(1)
    lgf_r, lgb_r = per_row(0), per_row(1)

    pos_r = lax.broadcasted_iota(jnp.int32, (c, 1), 0).astype(F32)
    wk_f = jnp.exp(lgf_l * (c - 1.0 - pos_r))
    wq_f = jnp.exp(lgf_l * (pos_r + 1.0))
    wk_b = jnp.exp(lgb_l * pos_r)
    wq_b = jnp.exp(lgb_l * (c - pos_r))
    dc_f = jnp.exp(lgf_r * float(c))
    dc_b = jnp.exp(lgb_r * float(c))

    tt = lax.broadcasted_iota(jnp.int32, (c, c), 0)
    ss = lax.broadcasted_iota(jnp.int32, (c, c), 1)
    dist = (tt - ss).astype(F32)

    def dmask(hh):
        lf = -jnp.exp(jnp.full((1, 1), dec_ref[0, 2 * j + hh], F32))
        lb = -jnp.exp(jnp.full((1, 1), dec_ref[1, 2 * j + hh], F32))
        return jnp.where(dist >= 0, jnp.exp(lf * jnp.maximum(dist, 0.0)),
                         jnp.exp(lb * jnp.maximum(-dist, 0.0)))

    kscale = RET_HEAD_DIM ** -0.5
    dcat = jnp.concatenate([dmask(0), dmask(1)], axis=1) * kscale
    wk_f = wk_f * kscale
    wk_b = wk_b * kscale
    wq_f = wq_f.astype(BF16)
    wq_b = wq_b.astype(BF16)
    bd = (first_r == first_l).astype(F32).astype(BF16)
    m0f = first_l.astype(F32)
    m0 = m0f.astype(BF16)
    m1 = (1.0 - m0f).astype(BF16)

    tn = (((0,), (0,)), ((), ()))
    nt = (((1,), (1,)), ((), ()))
    assert n_chunks % group == 0

    def chunk_local(i, carry):
        for u in range(group):
            n = i * group + u
            r0 = pl.multiple_of(n * c, c)
            qb = q_ref[pl.ds(r0, c), :]
            kb = k_ref[pl.ds(r0, c), :]
            vb = v_ref[pl.ds(r0, c), :]
            kf = kb.astype(F32)
            kvf_ref[n] = lax.dot_general((kf * wk_f).astype(BF16), vb, tn, preferred_element_type=F32)
            kvb_ref[n] = lax.dot_general((kf * wk_b).astype(BF16), vb, tn, preferred_element_type=F32)
            kcat = jnp.concatenate([kb * m0, kb * m1], axis=0)
            s = lax.dot_general(qb, kcat, nt, preferred_element_type=F32) * dcat
            a_ref[n] = jnp.concatenate([s.astype(BF16), qb * wq_f, qb * wq_b], axis=1)
        return carry

    lax.fori_loop(0, n_chunks // group, chunk_local, 0)

    def scan(n, carry):
        rf, rb = carry
        nb = n_chunks - 1 - n
        rf_ref[n] = rf.astype(BF16)
        rb_ref[nb] = rb.astype(BF16)
        return dc_f * rf + kvf_ref[n], dc_b * rb + kvb_ref[nb]

    zero = jnp.zeros((LANES, LANES), F32)
    lax.fori_loop(0, n_chunks, scan, (zero, zero))

    gn = gn_ref[...]

    def outputs(i, carry):
        for u in range(group):
            n = i * group + u
            r0 = pl.multiple_of(n * c, c)
            vb = v_ref[pl.ds(r0, c), :]
            bm = jnp.concatenate([vb * m0, vb * m1, rf_ref[n] * bd, rb_ref[n] * bd], axis=0)
            o = jnp.dot(a_ref[n], bm, preferred_element_type=F32)
            o2 = o * o
            ms0 = jnp.sum(o2 * m0f, axis=-1, keepdims=True)
            ms1 = jnp.sum(o2 * (1.0 - m0f), axis=-1, keepdims=True)
            ms = jnp.where(first_l, ms0, ms1) * (1.0 / hd)
            y = o * lax.rsqrt(ms + EPS) * gn
            gate = g_ref[pl.ds(r0, c), :].astype(F32)
            y = y * (gate * jax.nn.sigmoid(gate))
            o_ref[pl.ds(r0, c), :] = y.astype(BF16)
        return carry

    lax.fori_loop(0, n_chunks // group, outputs, 0)


def _retention(dec, proj3, gn_lanes, ret_width, chunk, group=8):
    b, s, _ = proj3.shape
    npair = ret_width // LANES
    assert chunk == LANES
    blk = lambda o: pl.BlockSpec((None, s, LANES), lambda i, j, o=o: (i, 0, o * npair + j))
    return pl.pallas_call(
        functools.partial(_retention_kernel, chunk=chunk, group=group),
        grid=(b, npair),
        in_specs=[
            pl.BlockSpec(memory_space=pltpu.SMEM),
            blk(0), blk(1), blk(2), blk(3),
            pl.BlockSpec((1, LANES), lambda i, j: (0, 0)),
        ],
        out_specs=pl.BlockSpec((None, s, LANES), lambda i, j: (i, 0, j)),
        out_shape=jax.ShapeDtypeStruct((b, s, ret_width), BF16),
        scratch_shapes=(
            [pltpu.VMEM((s // chunk, LANES, LANES), F32)] * 2
            + [pltpu.VMEM((s // chunk, LANES, LANES), BF16)] * 2
            + [pltpu.VMEM((s // chunk, chunk, 4 * chunk), BF16)]
        ),
        compiler_params=_cparams(("parallel", "parallel")),
        name="retention",
    )(dec, proj3, proj3, proj3, proj3, gn_lanes)


def _post_kernel(x_ref, r_ref, a_ref, wo_ref, g_ref, wg_ref, wu_ref, wd_ref, o_ref, *, f_chunks):
    mix = jnp.concatenate([r_ref[...], a_ref[...]], axis=1)
    x1 = x_ref[...] + jnp.dot(mix, wo_ref[...], preferred_element_type=F32)
    h = (x1 * lax.rsqrt(jnp.mean(x1 * x1, axis=-1, keepdims=True) + EPS) * g_ref[...]).astype(BF16)
    acc = x1
    for f0, f1 in f_chunks:
        gate = jnp.dot(h, wg_ref[:, f0:f1], preferred_element_type=F32)
        up = jnp.dot(h, wu_ref[:, f0:f1], preferred_element_type=F32)
        act = (gate * jax.nn.sigmoid(gate) * up).astype(BF16)
        acc = acc + jnp.dot(act, wd_ref[f0:f1, :], preferred_element_type=F32)
    o_ref[...] = acc


def _f_chunks(d_ff, mxu_cols=256, max_cols=1536):
    out, f0 = [], 0
    while f0 < d_ff:
        f1 = min(d_ff, f0 + max_cols)
        out.append((f0, f1))
        f0 = f1
    assert all((a % mxu_cols == 0) for a, _ in out)
    return tuple(out)


def _post(x2, ret2, da2, wo, g, wg, wu, wd, layer, tm):
    m, d = x2.shape
    rw, aw = ret2.shape[1], da2.shape[1]
    d_ff = wg.shape[2]
    weight = lambda rows, cols: pl.BlockSpec((None, rows, cols), lambda i: (layer, 0, 0),
                                             pipeline_mode=pl.Buffered(1))
    return pl.pallas_call(
        functools.partial(_post_kernel, f_chunks=_f_chunks(d_ff)),
        grid=(m // tm,),
        in_specs=[
            pl.BlockSpec((tm, d), lambda i: (i, 0)),
            pl.BlockSpec((tm, rw), lambda i: (i, 0)),
            pl.BlockSpec((tm, aw), lambda i: (i, 0)),
            weight(rw + aw, d),
            pl.BlockSpec((1, d), lambda i: (0, 0), pipeline_mode=pl.Buffered(1)),
            weight(d, d_ff),
            weight(d, d_ff),
            weight(d_ff, d),
        ],
        out_specs=pl.BlockSpec((tm, d), lambda i: (i, 0)),
        out_shape=jax.ShapeDtypeStruct((m, d), F32),
        compiler_params=_cparams(("parallel",)),
        name="post",
    )(x2, ret2, da2, wo, g, wg, wu, wd)


def _slope_pieces(n_heads):
    slopes = (2.0 ** (-8.0 * np.arange(1, n_heads + 1, dtype=np.float64) / n_heads) * LOG2E).astype(np.float32)
    cols = np.zeros((n_heads, 2, DIFF_HEAD_DIM, 1), np.float32)
    rem = slopes.astype(np.float64)
    for p in range(N_PIECES):
        piece = rem.astype(BF16).astype(np.float64)
        for dgt in range(N_DIGITS):
            cols[:, 0, N_DIGITS * p + dgt, 0] = piece
            cols[:, 1, N_AUG // 2 + N_DIGITS * p + dgt, 0] = piece
        rem = rem - piece
    return tuple(float(v) for v in slopes), jnp.asarray(cols)


def kernel(x, attn_norm_g, w_in, ret_decay_fwd, ret_decay_bwd, ret_norm_g, dq_norm_g, dk_norm_g,
           lambda_q1, lambda_k1, lambda_q2, lambda_k2, diff_norm_g, w_out, ffn_norm_g,
           w_gate, w_up, w_down):
    b, s, d = x.shape
    depth = w_in.shape[0]
    ret_width = d // 2
    diff_width = d - ret_width
    n_dheads = diff_width // DIFF_V_DIM
    tm = 512
    tq = min(1024, s)
    tk = 512
    chunk = 128
    slopes_l2, cpieces = _slope_pieces(n_dheads)

    wo_b, wg_b, wu_b, wd_b = (w.astype(BF16) for w in (w_out, w_gate, w_up, w_down))

    x2 = x.reshape(b * s, d)
    for l in range(depth):
        lam_init = 0.8 - 0.6 * math.exp(-0.3 * l)
        pr, qt, ka, vt, qn2, kn2 = _in_proj(
            x2.reshape(b, s, d), attn_norm_g[l][None], w_in, l, dq_norm_g[l].astype(F32)[:, None],
            dk_norm_g[l].astype(F32)[:, None], cpieces, n_dheads, 4 * ret_width, tm)

        dec = jnp.stack([ret_decay_fwd[l], ret_decay_bwd[l]]).astype(F32)
        gn_lanes = jnp.tile(ret_norm_g[l].astype(F32), LANES // RET_HEAD_DIM)[None]
        ret = _retention(dec, pr, gn_lanes, ret_width, chunk)

        lam_params = jnp.stack([lambda_q1[l], lambda_k1[l], lambda_q2[l], lambda_k2[l]]).astype(F32)
        da = _diffattn(lam_params, qt, ka, vt, qn2, kn2, diff_norm_g[l].astype(F32)[:, None],
                       tq, tk, lam_init, slopes_l2)

        x2 = _post(x2, ret.reshape(b * s, ret_width), da.reshape(b * s, diff_width),
                   wo_b, ffn_norm_g[l][None], wg_b, wu_b, wd_b, l, tm)
    return x2.reshape(b, s, d)
```

```python
import functools
import math

import numpy as np
import jax
import jax.numpy as jnp
from jax import lax
from jax.experimental import pallas as pl
from jax.experimental.pallas import tpu as pltpu

EPS = 1e-6
LOG2E = 1.4426950408889634

RET_HEAD_DIM = 64
DIFF_HEAD_DIM = 64
DIFF_V_DIM = 128
LANES = 128
VMEM_LIMIT = 56 * 1024 * 1024

BF16 = jnp.bfloat16
F32 = jnp.float32


def _cparams(sem):
    return pltpu.CompilerParams(dimension_semantics=sem, vmem_limit_bytes=VMEM_LIMIT)


N_DIGITS = 3
N_PIECES = 3
N_AUG = 2 * N_DIGITS * N_PIECES
POS_RADIX = 32
CAST_COLS = 512


def _pos_digits(pos):
    d0 = pos % POS_RADIX
    d1 = (pos // POS_RADIX) % POS_RADIX * POS_RADIX
    d2 = pos // (POS_RADIX * POS_RADIX) * (POS_RADIX * POS_RADIX)
    return [d.astype(F32) for d in (d0, d1, d2)]


def _in_proj_kernel(x_ref, g_ref, w_ref, gq_ref, gk_ref, cp_ref,
                    pr_ref, qt_ref, ka_ref, vt_ref, qn2_ref, kn2_ref, wb_ref, *, n_heads, ret_cols):
    ts = x_ref.shape[0]
    ng = 2 * n_heads
    hd = DIFF_HEAD_DIM
    half = N_AUG // 2
    aw = n_heads * DIFF_V_DIM

    @pl.when((pl.program_id(0) == 0) & (pl.program_id(1) == 0))
    def _():
        for c0 in range(0, w_ref.shape[1], CAST_COLS):
            wb_ref[:, c0:c0 + CAST_COLS] = w_ref[:, c0:c0 + CAST_COLS].astype(BF16)

    x = x_ref[...]
    y = (x * lax.rsqrt(jnp.mean(x * x, axis=-1, keepdims=True) + EPS) * g_ref[...]).astype(BF16)
    dqkv = jnp.dot(y, wb_ref[:, ret_cols:], preferred_element_type=F32)
    pr_ref[...] = jnp.dot(y, wb_ref[:, :ret_cols], preferred_element_type=F32).astype(BF16)

    def head_norm_t(xf, g_col, scale):
        xt = xf.T.reshape(ng, hd, ts)
        ms = jnp.mean(xt * xt, axis=1, keepdims=True)
        return (xt * lax.rsqrt(ms + EPS) * (g_col[None] * scale)).astype(BF16)

    qn = head_norm_t(dqkv[:, :aw], gq_ref[...], DIFF_HEAD_DIM ** -0.5 * LOG2E)
    kn = head_norm_t(dqkv[:, aw:2 * aw], gk_ref[...], 1.0)
    qn2_ref[...] = jnp.sum(jnp.square(qn.astype(F32)), axis=1, keepdims=True)
    kn2_ref[...] = jnp.sum(jnp.square(kn.astype(F32)), axis=1, keepdims=True)

    pos = pl.program_id(1) * ts + lax.broadcasted_iota(jnp.int32, (1, ts), 1)
    d0, d1, d2 = _pos_digits(pos)
    r = lax.broadcasted_iota(jnp.int32, (hd, 1), 0)
    which = jnp.where(r < half, r, r - half) % N_DIGITS
    digit = jnp.where(which == 0, d0, jnp.where(which == 1, d1, d2))
    k_digits = jnp.where(r < half, digit, 0.0)
    q_digits = jnp.where((r >= half) & (r < N_AUG), -digit, 0.0)
    for g in range(ng):
        q_rows = (q_digits + cp_ref[g // 2, 0]).astype(BF16)
        k_rows = (k_digits + cp_ref[g // 2, 1]).astype(BF16)
        qt_ref[g] = jnp.concatenate([qn[g], q_rows], axis=0)
        ka_ref[g] = jnp.concatenate([kn[g], k_rows], axis=0).astype(F32).T.astype(BF16)
    vt_ref[...] = dqkv[:, 2 * aw:].T.reshape(n_heads, DIFF_V_DIM, ts).astype(BF16)


def _in_proj(x3, g, w_all, layer, gq, gk, cpieces, n_heads, ret_cols, ts):
    b, s, d = x3.shape
    n = w_all.shape[2]
    ng = 2 * n_heads
    assert n == ret_cols + 3 * n_heads * DIFF_V_DIM and n % CAST_COLS == 0
    const = lambda shape: pl.BlockSpec(shape, lambda i, j: (0,) * len(shape), pipeline_mode=pl.Buffered(1))
    return pl.pallas_call(
        functools.partial(_in_proj_kernel, n_heads=n_heads, ret_cols=ret_cols),
        grid=(b, s // ts),
        in_specs=[
            pl.BlockSpec((None, ts, d), lambda i, j: (i, j, 0)),
            const((1, d)),
            pl.BlockSpec((None, d, n), lambda i, j: (layer, 0, 0), pipeline_mode=pl.Buffered(1)),
            const((DIFF_HEAD_DIM, 1)),
            const((DIFF_HEAD_DIM, 1)),
            const((n_heads, 2, DIFF_HEAD_DIM, 1)),
        ],
        out_specs=[
            pl.BlockSpec((None, ts, ret_cols), lambda i, j: (i, j, 0)),
            pl.BlockSpec((None, ng, LANES, ts), lambda i, j: (i, 0, 0, j)),
            pl.BlockSpec((None, ng, ts, LANES), lambda i, j: (i, 0, j, 0)),
            pl.BlockSpec((None, n_heads, DIFF_V_DIM, ts), lambda i, j: (i, 0, 0, j)),
            pl.BlockSpec((None, ng, 1, ts), lambda i, j: (i, 0, 0, j)),
            pl.BlockSpec((None, ng, 1, ts), lambda i, j: (i, 0, 0, j)),
        ],
        out_shape=[
            jax.ShapeDtypeStruct((b, s, ret_cols), BF16),
            jax.ShapeDtypeStruct((b, ng, LANES, s), BF16),
            jax.ShapeDtypeStruct((b, ng, s, LANES), BF16),
            jax.ShapeDtypeStruct((b, n_heads, DIFF_V_DIM, s), BF16),
            jax.ShapeDtypeStruct((b, ng, 1, s), F32),
            jax.ShapeDtypeStruct((b, ng, 1, s), F32),
        ],
        scratch_shapes=[pltpu.VMEM((d, n), BF16)],
        compiler_params=_cparams(("arbitrary", "arbitrary")),
        name="in_proj",
    )(x3, g, w_all, gq, gk, cpieces)


NEG_BIG = -1e30
SAFE_LOG2_RANGE = 96.0


def _diffattn_kernel(lam_ref, qt_ref, ka_ref, vt_ref, qn2_ref, kn2_ref, gd_ref, o_ref,
                     qs_ref, corr_ref, m_ref, l_ref, acc_ref, s_refs, p_refs, bmax_refs, kmax_ref,
                     *, tq, tk, n_sub, lam_init, slopes_l2):
    h = pl.program_id(1)
    step = pl.program_id(2)
    s_len = ka_ref.shape[1]
    nk = s_len // tk
    ratio = tq // tk
    assert tq % tk == 0 and nk % 2 == 0
    slope = jnp.float32(0.0)
    for hh, sv in enumerate(slopes_l2):
        slope = jnp.where(h == hh, jnp.float32(sv), slope)

    @pl.when(step == 0)
    def _():
        ii = lax.broadcasted_iota(jnp.int32, (tk, tq), 0)
        jj = lax.broadcasted_iota(jnp.int32, (tk, tq), 1)
        corr_ref[0] = jnp.zeros((tk, tq), F32)
        for d in range(ratio):
            corr_ref[1 + d] = (2.0 * slope) * jnp.maximum(ii - jj + d * tk, 0).astype(F32)
        for mp in range(2):
            kmax_ref[mp] = jnp.max(kn2_ref[mp])

    la = lam_ref[...]
    lam = (jnp.exp(jnp.sum(la[0:1] * la[1:2], axis=-1, keepdims=True))
           - jnp.exp(jnp.sum(la[2:3] * la[3:4], axis=-1, keepdims=True)) + lam_init)
    out_gain = gd_ref[...] * (1.0 - lam_init)
    row = lax.broadcasted_iota(jnp.int32, (LANES, 1), 0)
    sign = jnp.where(row < DIFF_HEAD_DIM, 1.0, -1.0).astype(BF16)

    def load_queries(sub):
        for mp in range(2):
            q = qt_ref[mp, :, sub * tq:(sub + 1) * tq]
            qs_ref[sub, 0, mp] = q
            qs_ref[sub, 1, mp] = q * sign

    def finish(sub):
        o1 = acc_ref[sub, 0] * (1.0 / l_ref[sub, 0])
        o2 = acc_ref[sub, 1] * (lam / l_ref[sub, 1])
        da = o1 - o2
        y = da * lax.rsqrt(jnp.mean(da * da, axis=0, keepdims=True) + EPS) * out_gain
        o_ref[sub * tq:(sub + 1) * tq, :] = y.T.astype(BF16)

    def block_start(j):
        return pl.multiple_of(j * tk, tk)

    def weighted_values(j, sub, slot):
        return jnp.dot(vt_ref[:, pl.ds(block_start(j), tk)], p_refs[sub][slot][...],
                       preferred_element_type=F32)

    def add_values(pv, sub, first=False):
        for mp in range(2):
            part = pv[:, mp * tq:(mp + 1) * tq]
            acc_ref[sub, mp] = part if first else acc_ref[sub, mp] + part

    def bounded_path():
        for sub in range(n_sub):
            load_queries(sub)
            first_block = ratio * (n_sub * step + sub)

            def block_of(r):
                j = first_block + r
                return jnp.where(j >= nk, j - nk, j)

            for r in range(nk):
                slot = r % 2
                j = block_of(r)
                if r > 0:
                    pv = weighted_values(block_of(r - 1), sub, 1 - slot)
                above = 0 if r < ratio else (first_block + r < nk).astype(jnp.int32)
                for mp in range(2):
                    k = ka_ref[mp, pl.ds(block_start(j), tk), :]
                    s = jnp.dot(k, qs_ref[sub, above, mp], preferred_element_type=F32)
                    if r < ratio:
                        s = s - corr_ref[1 + r]
                    p = jnp.exp2(s)
                    psum = jnp.sum(p, axis=0, keepdims=True)
                    l_ref[sub, mp] = psum if r == 0 else l_ref[sub, mp] + psum
                    p_refs[sub][slot][:, mp * tq:(mp + 1) * tq] = p.astype(BF16)
                if r > 0:
                    add_values(pv, sub, first=r == 1)
            add_values(weighted_values(block_of(nk - 1), sub, (nk - 1) % 2), sub)
            finish(sub)

    def online_max_path():
        for sub in range(n_sub):
            load_queries(sub)
            qi = n_sub * step + sub
            m_ref[...] = jnp.full(m_ref.shape, NEG_BIG, F32)
            l_ref[sub] = jnp.zeros(l_ref.shape[1:], F32)
            acc_ref[sub] = jnp.zeros(acc_ref.shape[1:], F32)
            p_refs[sub][1][...] = jnp.zeros(p_refs[sub][1].shape, BF16)

            def scores(j, slot):
                d = j - ratio * qi
                above = (d >= ratio).astype(jnp.int32)
                overlap = jnp.where((d >= 0) & (d < ratio), d + 1, 0)
                for mp in range(2):
                    k = ka_ref[mp, pl.ds(block_start(j), tk), :]
                    s = jnp.dot(k, qs_ref[sub, above, mp], preferred_element_type=F32) - corr_ref[overlap]
                    s_refs[slot][mp] = s
                    bmax_refs[slot][mp] = jnp.max(s, axis=0, keepdims=True)

            def region(j, slot):
                pv = weighted_values(jnp.maximum(j - 1, 0), sub, 1 - slot)
                alphas = []
                for mp in range(2):
                    m_old = m_ref[mp]
                    m_new = jnp.maximum(m_old, bmax_refs[slot][mp])
                    p = jnp.exp2(s_refs[slot][mp] - m_new)
                    alpha = jnp.exp2(m_old - m_new)
                    l_ref[sub, mp] = alpha * l_ref[sub, mp] + jnp.sum(p, axis=0, keepdims=True)
                    m_ref[mp] = m_new
                    p_refs[sub][slot][:, mp * tq:(mp + 1) * tq] = p.astype(BF16)
                    alphas.append(alpha)
                scores(jnp.minimum(j + 1, nk - 1), 1 - slot)
                for mp in range(2):
                    acc_ref[sub, mp] = (acc_ref[sub, mp] + pv[:, mp * tq:(mp + 1) * tq]) * alphas[mp]

            scores(0, 0)

            def pair(i, c):
                region(2 * i, 0)
                region(2 * i + 1, 1)
                return c

            lax.fori_loop(0, nk // 2, pair, 0)
            add_values(weighted_values(nk - 1, sub, 1), sub)
            finish(sub)

    bound2 = jnp.float32(0.0)
    for mp in range(2):
        bound2 = jnp.maximum(bound2, jnp.max(qn2_ref[mp]) * kmax_ref[mp])
    lax.cond(bound2 <= SAFE_LOG2_RANGE ** 2, bounded_path, online_max_path)


def _diffattn(lam_params, qt, ka, vt, qn2, kn2, gd_col, tq, tk, n_sub, lam_init, slopes_l2):
    b, ng, _, s = qt.shape
    n_heads = ng // 2
    tqs = n_sub * tq
    kern = functools.partial(_diffattn_kernel, tq=tq, tk=tk, n_sub=n_sub, lam_init=lam_init,
                             slopes_l2=slopes_l2)
    return pl.pallas_call(
        kern,
        grid=(b, n_heads, s // tqs),
        in_specs=[
            pl.BlockSpec((4, DIFF_HEAD_DIM), lambda i, h, q: (0, 0)),
            pl.BlockSpec((None, 2, LANES, tqs), lambda i, h, q: (i, h, 0, q)),
            pl.BlockSpec((None, 2, s, LANES), lambda i, h, q: (i, h, 0, 0)),
            pl.BlockSpec((None, None, DIFF_V_DIM, s), lambda i, h, q: (i, h, 0, 0)),
            pl.BlockSpec((None, 2, 1, tqs), lambda i, h, q: (i, h, 0, q)),
            pl.BlockSpec((None, 2, 1, s), lambda i, h, q: (i, h, 0, 0)),
            pl.BlockSpec((DIFF_V_DIM, 1), lambda i, h, q: (0, 0)),
        ],
        out_specs=pl.BlockSpec((None, tqs, DIFF_V_DIM), lambda i, h, q: (i, q, h)),
        out_shape=jax.ShapeDtypeStruct((b, s, n_heads * DIFF_V_DIM), BF16),
        scratch_shapes=[
            pltpu.VMEM((n_sub, 2, 2, LANES, tq), BF16),
            pltpu.VMEM((1 + tq // tk, tk, tq), F32),
            pltpu.VMEM((2, 1, tq), F32),
            pltpu.VMEM((n_sub, 2, 1, tq), F32),
            pltpu.VMEM((n_sub, 2, DIFF_V_DIM, tq), F32),
            [pltpu.VMEM((2, tk, tq), F32)] * 2,
            [[pltpu.VMEM((tk, 2 * tq), BF16)] * 2] * n_sub,
            [pltpu.VMEM((2, 1, tq), F32)] * 2,
            pltpu.SMEM((2,), F32),
        ],
        compiler_params=_cparams(("parallel", "parallel", "arbitrary")),
        name="diffattn",
    )(lam_params, qt, ka, vt, qn2, kn2, gd_col)


def _retention_kernel(dec_ref, q_ref, k_ref, v_ref, g_ref, gn_ref, o_ref,
                      kvf_ref, kvb_ref, rf_ref, rb_ref, a_ref, *, chunk, group):
    j = pl.program_id(1)
    s_len = q_ref.shape[0]
    n_chunks = s_len // chunk
    hd = RET_HEAD_DIM
    c = chunk

    lane = lax.broadcasted_iota(jnp.int32, (1, LANES), 1)
    rowi = lax.broadcasted_iota(jnp.int32, (LANES, 1), 0)
    first_l = lane < hd
    first_r = rowi < hd

    def per_lane(d):
        return -jnp.exp(jnp.where(first_l, dec_ref[d, 2 * j], dec_ref[d, 2 * j + 1]))

    def per_row(d):
        return -jnp.exp(jnp.where(first_r, dec_ref[d, 2 * j], dec_ref[d, 2 * j + 1]))

    lgf_l, lgb_l = per_lane(0), per_lane(1)
    lgf_r, lgb_r = per_row(0), per_row(1)

    pos_r = lax.broadcasted_iota(jnp.int32, (c, 1), 0).astype(F32)
    wk_f = jnp.exp(lgf_l * (c - 1.0 - pos_r))
    wq_f = jnp.exp(lgf_l * (pos_r + 1.0))
    wk_b = jnp.exp(lgb_l * pos_r)
    wq_b = jnp.exp(lgb_l * (c - pos_r))
    dc_f = jnp.exp(lgf_r * float(c))
    dc_b = jnp.exp(lgb_r * float(c))

    tt = lax.broadcasted_iota(jnp.int32, (c, c), 0)
    ss = lax.broadcasted_iota(jnp.int32, (c, c), 1)
    dist = (tt - ss).astype(F32)

    def dmask(hh):
        lf = -jnp.exp(jnp.full((1, 1), dec_ref[0, 2 * j + hh], F32))
        lb = -jnp.exp(jnp.full((1, 1), dec_ref[1, 2 * j + hh], F32))
        return jnp.where(dist >= 0, jnp.exp(lf * jnp.maximum(dist, 0.0)),
                         jnp.exp(lb * jnp.maximum(-dist, 0.0)))

    kscale = RET_HEAD_DIM ** -0.5
    dcat = jnp.concatenate([dmask(0), dmask(1)], axis=1) * kscale
    wk_f = wk_f * kscale
    wk_b = wk_b * kscale
    wq_f = wq_f.astype(BF16)
    wq_b = wq_b.astype(BF16)
    bd = (first_r == first_l).astype(F32).astype(BF16)
    m0f = first_l.astype(F32)
    m0 = m0f.astype(BF16)
    m1 = (1.0 - m0f).astype(BF16)

    tn = (((0,), (0,)), ((), ()))
    nt = (((1,), (1,)), ((), ()))
    assert n_chunks % group == 0

    def chunk_local(i, carry):
        for u in range(group):
            n = i * group + u
            r0 = pl.multiple_of(n * c, c)
            qb = q_ref[pl.ds(r0, c), :]
            kb = k_ref[pl.ds(r0, c), :]
            vb = v_ref[pl.ds(r0, c), :]
            kf = kb.astype(F32)
            kvf_ref[n] = lax.dot_general((kf * wk_f).astype(BF16), vb, tn, preferred_element_type=F32)
            kvb_ref[n] = lax.dot_general((kf * wk_b).astype(BF16), vb, tn, preferred_element_type=F32)
            kcat = jnp.concatenate([kb * m0, kb * m1], axis=0)
            s = lax.dot_general(qb, kcat, nt, preferred_element_type=F32) * dcat
            a_ref[n] = jnp.concatenate([s.astype(BF16), qb * wq_f, qb * wq_b], axis=1)
        return carry

    lax.fori_loop(0, n_chunks // group, chunk_local, 0)

    def scan(n, carry):
        rf, rb = carry
        nb = n_chunks - 1 - n
        rf_ref[n] = rf.astype(BF16)
        rb_ref[nb] = rb.astype(BF16)
        return dc_f * rf + kvf_ref[n], dc_b * rb + kvb_ref[nb]

    zero = jnp.zeros((LANES, LANES), F32)
    lax.fori_loop(0, n_chunks, scan, (zero, zero))

    gn = gn_ref[...]

    def outputs(i, carry):
        for u in range(group):
            n = i * group + u
            r0 = pl.multiple_of(n * c, c)
            vb = v_ref[pl.ds(r0, c), :]
            bm = jnp.concatenate([vb * m0, vb * m1, rf_ref[n] * bd, rb_ref[n] * bd], axis=0)
            o = jnp.dot(a_ref[n], bm, preferred_element_type=F32)
            o2 = o * o
            ms0 = jnp.sum(o2 * m0f, axis=-1, keepdims=True)
            ms1 = jnp.sum(o2 * (1.0 - m0f), axis=-1, keepdims=True)
            ms = jnp.where(first_l, ms0, ms1) * (1.0 / hd)
            y = o * lax.rsqrt(ms + EPS) * gn
            gate = g_ref[pl.ds(r0, c), :].astype(F32)
            y = y * (gate * jax.nn.sigmoid(gate))
            o_ref[pl.ds(r0, c), :] = y.astype(BF16)
        return carry

    lax.fori_loop(0, n_chunks // group, outputs, 0)


def _retention(dec, proj3, gn_lanes, ret_width, chunk, group=8):
    b, s, _ = proj3.shape
    npair = ret_width // LANES
    assert chunk == LANES
    blk = lambda o: pl.BlockSpec((None, s, LANES), lambda i, j, o=o: (i, 0, o * npair + j))
    return pl.pallas_call(
        functools.partial(_retention_kernel, chunk=chunk, group=group),
        grid=(b, npair),
        in_specs=[
            pl.BlockSpec(memory_space=pltpu.SMEM),
            blk(0), blk(1), blk(2), blk(3),
            pl.BlockSpec((1, LANES), lambda i, j: (0, 0)),
        ],
        out_specs=pl.BlockSpec((None, s, LANES), lambda i, j: (i, 0, j)),
        out_shape=jax.ShapeDtypeStruct((b, s, ret_width), BF16),
        scratch_shapes=(
            [pltpu.VMEM((s // chunk, LANES, LANES), F32)] * 2
            + [pltpu.VMEM((s // chunk, LANES, LANES), BF16)] * 2
            + [pltpu.VMEM((s // chunk, chunk, 4 * chunk), BF16)]
        ),
        compiler_params=_cparams(("parallel", "parallel")),
        name="retention",
    )(dec, proj3, proj3, proj3, proj3, gn_lanes)


def _post_kernel(x_ref, r_ref, a_ref, wo_ref, g_ref, wg_ref, wu_ref, wd_ref, o_ref, *, f_chunks):
    mix = jnp.concatenate([r_ref[...], a_ref[...]], axis=1)
    x1 = x_ref[...] + jnp.dot(mix, wo_ref[...], preferred_element_type=F32)
    h = (x1 * lax.rsqrt(jnp.mean(x1 * x1, axis=-1, keepdims=True) + EPS) * g_ref[...]).astype(BF16)
    acc = x1
    for f0, f1 in f_chunks:
        gate = jnp.dot(h, wg_ref[:, f0:f1], preferred_element_type=F32)
        up = jnp.dot(h, wu_ref[:, f0:f1], preferred_element_type=F32)
        act = (gate * jax.nn.sigmoid(gate) * up).astype(BF16)
        acc = acc + jnp.dot(act, wd_ref[f0:f1, :], preferred_element_type=F32)
    o_ref[...] = acc


def _f_chunks(d_ff, mxu_cols=256, max_cols=1536):
    out, f0 = [], 0
    while f0 < d_ff:
        f1 = min(d_ff, f0 + max_cols)
        out.append((f0, f1))
        f0 = f1
    assert all((a % mxu_cols == 0) for a, _ in out)
    return tuple(out)


def _post(x2, ret2, da2, wo, g, wg, wu, wd, layer, tm):
    m, d = x2.shape
    rw, aw = ret2.shape[1], da2.shape[1]
    d_ff = wg.shape[2]
    weight = lambda rows, cols: pl.BlockSpec((None, rows, cols), lambda i: (layer, 0, 0),
                                             pipeline_mode=pl.Buffered(1))
    return pl.pallas_call(
        functools.partial(_post_kernel, f_chunks=_f_chunks(d_ff)),
        grid=(m // tm,),
        in_specs=[
            pl.BlockSpec((tm, d), lambda i: (i, 0)),
            pl.BlockSpec((tm, rw), lambda i: (i, 0)),
            pl.BlockSpec((tm, aw), lambda i: (i, 0)),
            weight(rw + aw, d),
            pl.BlockSpec((1, d), lambda i: (0, 0), pipeline_mode=pl.Buffered(1)),
            weight(d, d_ff),
            weight(d, d_ff),
            weight(d_ff, d),
        ],
        out_specs=pl.BlockSpec((tm, d), lambda i: (i, 0)),
        out_shape=jax.ShapeDtypeStruct((m, d), F32),
        compiler_params=_cparams(("parallel",)),
        name="post",
    )(x2, ret2, da2, wo, g, wg, wu, wd)


def _slope_pieces(n_heads):
    slopes = (2.0 ** (-8.0 * np.arange(1, n_heads + 1, dtype=np.float64) / n_heads) * LOG2E).astype(np.float32)
    cols = np.zeros((n_heads, 2, DIFF_HEAD_DIM, 1), np.float32)
    rem = slopes.astype(np.float64)
    for p in range(N_PIECES):
        piece = rem.astype(BF16).astype(np.float64)
        for dgt in range(N_DIGITS):
            cols[:, 0, N_DIGITS * p + dgt, 0] = piece
            cols[:, 1, N_AUG // 2 + N_DIGITS * p + dgt, 0] = piece
        rem = rem - piece
    return tuple(float(v) for v in slopes), jnp.asarray(cols)


def kernel(x, attn_norm_g, w_in, ret_decay_fwd, ret_decay_bwd, ret_norm_g, dq_norm_g, dk_norm_g,
           lambda_q1, lambda_k1, lambda_q2, lambda_k2, diff_norm_g, w_out, ffn_norm_g,
           w_gate, w_up, w_down):
    b, s, d = x.shape
    depth = w_in.shape[0]
    ret_width = d // 2
    diff_width = d - ret_width
    n_dheads = diff_width // DIFF_V_DIM
    tm = 512
    tq = min(1024, s)
    tk = 512
    n_sub = 2 if s % (2 * tq) == 0 else 1
    chunk = 128
    slopes_l2, cpieces = _slope_pieces(n_dheads)

    wo_b, wg_b, wu_b, wd_b = (w.astype(BF16) for w in (w_out, w_gate, w_up, w_down))

    x2 = x.reshape(b * s, d)
    for l in range(depth):
        lam_init = 0.8 - 0.6 * math.exp(-0.3 * l)
        pr, qt, ka, vt, qn2, kn2 = _in_proj(
            x2.reshape(b, s, d), attn_norm_g[l][None], w_in, l, dq_norm_g[l].astype(F32)[:, None],
            dk_norm_g[l].astype(F32)[:, None], cpieces, n_dheads, 4 * ret_width, tm)

        dec = jnp.stack([ret_decay_fwd[l], ret_decay_bwd[l]]).astype(F32)
        gn_lanes = jnp.tile(ret_norm_g[l].astype(F32), LANES // RET_HEAD_DIM)[None]
        ret = _retention(dec, pr, gn_lanes, ret_width, chunk)

        lam_params = jnp.stack([lambda_q1[l], lambda_k1[l], lambda_q2[l], lambda_k2[l]]).astype(F32)
        da = _diffattn(lam_params, qt, ka, vt, qn2, kn2, diff_norm_g[l].astype(F32)[:, None],
                       tq, tk, n_sub, lam_init, slopes_l2)

        x2 = _post(x2, ret.reshape(b * s, ret_width), da.reshape(b * s, diff_width),
                   wo_b, ffn_norm_g[l][None], wg_b, wu_b, wd_b, l, tm)
    return x2.reshape(b, s, d)
```

```python
import functools
import math

import numpy as np
import jax
import jax.numpy as jnp
from jax import lax
from jax.experimental import pallas as pl
from jax.experimental.pallas import tpu as pltpu

EPS = 1e-6
LOG2E = 1.4426950408889634

RET_HEAD_DIM = 64
DIFF_HEAD_DIM = 64
DIFF_V_DIM = 128
LANES = 128
VMEM_LIMIT = 56 * 1024 * 1024

BF16 = jnp.bfloat16
F32 = jnp.float32


def _cparams(sem):
    return pltpu.CompilerParams(dimension_semantics=sem, vmem_limit_bytes=VMEM_LIMIT)


N_DIGITS = 3
N_PIECES = 3
N_AUG = 2 * N_DIGITS * N_PIECES
POS_RADIX = 32
CAST_COLS = 512


def _pos_digits(pos):
    d0 = pos % POS_RADIX
    d1 = (pos // POS_RADIX) % POS_RADIX * POS_RADIX
    d2 = pos // (POS_RADIX * POS_RADIX) * (POS_RADIX * POS_RADIX)
    return [d.astype(F32) for d in (d0, d1, d2)]


def _in_proj_kernel(x_ref, g_ref, w_ref, gq_ref, gk_ref, cp_ref,
                    pr_ref, qt_ref, ka_ref, vt_ref, qn2_ref, kn2_ref, wb_ref, *, n_heads, ret_cols):
    ts = x_ref.shape[0]
    ng = 2 * n_heads
    hd = DIFF_HEAD_DIM
    half = N_AUG // 2
    aw = n_heads * DIFF_V_DIM

    @pl.when((pl.program_id(0) == 0) & (pl.program_id(1) == 0))
    def _():
        for c0 in range(0, w_ref.shape[1], CAST_COLS):
            wb_ref[:, c0:c0 + CAST_COLS] = w_ref[:, c0:c0 + CAST_COLS].astype(BF16)

    x = x_ref[...]
    y = (x * lax.rsqrt(jnp.mean(x * x, axis=-1, keepdims=True) + EPS) * g_ref[...]).astype(BF16)
    dqkv = jnp.dot(y, wb_ref[:, ret_cols:], preferred_element_type=F32)
    pr_ref[...] = jnp.dot(y, wb_ref[:, :ret_cols], preferred_element_type=F32).astype(BF16)

    def head_norm_t(xf, g_col, scale):
        xt = xf.T.reshape(ng, hd, ts)
        ms = jnp.mean(xt * xt, axis=1, keepdims=True)
        return (xt * lax.rsqrt(ms + EPS) * (g_col[None] * scale)).astype(BF16)

    qn = head_norm_t(dqkv[:, :aw], gq_ref[...], DIFF_HEAD_DIM ** -0.5 * LOG2E)
    kn = head_norm_t(dqkv[:, aw:2 * aw], gk_ref[...], 1.0)
    qn2_ref[...] = jnp.sum(jnp.square(qn.astype(F32)), axis=1, keepdims=True)
    kn2_ref[...] = jnp.sum(jnp.square(kn.astype(F32)), axis=1, keepdims=True)

    pos = pl.program_id(1) * ts + lax.broadcasted_iota(jnp.int32, (1, ts), 1)
    d0, d1, d2 = _pos_digits(pos)
    r = lax.broadcasted_iota(jnp.int32, (hd, 1), 0)
    which = jnp.where(r < half, r, r - half) % N_DIGITS
    digit = jnp.where(which == 0, d0, jnp.where(which == 1, d1, d2))
    k_digits = jnp.where(r < half, digit, 0.0)
    q_digits = jnp.where((r >= half) & (r < N_AUG), -digit, 0.0)
    for g in range(ng):
        q_rows = (q_digits + cp_ref[g // 2, 0]).astype(BF16)
        k_rows = (k_digits + cp_ref[g // 2, 1]).astype(BF16)
        qt_ref[g] = jnp.concatenate([qn[g], q_rows], axis=0)
        ka_ref[g] = jnp.concatenate([kn[g], k_rows], axis=0).astype(F32).T.astype(BF16)
    vt_ref[...] = dqkv[:, 2 * aw:].T.reshape(n_heads, DIFF_V_DIM, ts).astype(BF16)


def _in_proj(x3, g, w_all, layer, gq, gk, cpieces, n_heads, ret_cols, ts):
    b, s, d = x3.shape
    n = w_all.shape[2]
    ng = 2 * n_heads
    assert n == ret_cols + 3 * n_heads * DIFF_V_DIM and n % CAST_COLS == 0
    const = lambda shape: pl.BlockSpec(shape, lambda i, j: (0,) * len(shape), pipeline_mode=pl.Buffered(1))
    return pl.pallas_call(
        functools.partial(_in_proj_kernel, n_heads=n_heads, ret_cols=ret_cols),
        grid=(b, s // ts),
        in_specs=[
            pl.BlockSpec((None, ts, d), lambda i, j: (i, j, 0)),
            const((1, d)),
            pl.BlockSpec((None, d, n), lambda i, j: (layer, 0, 0), pipeline_mode=pl.Buffered(1)),
            const((DIFF_HEAD_DIM, 1)),
            const((DIFF_HEAD_DIM, 1)),
            const((n_heads, 2, DIFF_HEAD_DIM, 1)),
        ],
        out_specs=[
            pl.BlockSpec((None, ts, ret_cols), lambda i, j: (i, j, 0)),
            pl.BlockSpec((None, ng, LANES, ts), lambda i, j: (i, 0, 0, j)),
            pl.BlockSpec((None, ng, ts, LANES), lambda i, j: (i, 0, j, 0)),
            pl.BlockSpec((None, n_heads, DIFF_V_DIM, ts), lambda i, j: (i, 0, 0, j)),
            pl.BlockSpec((None, ng, 1, ts), lambda i, j: (i, 0, 0, j)),
            pl.BlockSpec((None, ng, 1, ts), lambda i, j: (i, 0, 0, j)),
        ],
        out_shape=[
            jax.ShapeDtypeStruct((b, s, ret_cols), BF16),
            jax.ShapeDtypeStruct((b, ng, LANES, s), BF16),
            jax.ShapeDtypeStruct((b, ng, s, LANES), BF16),
            jax.ShapeDtypeStruct((b, n_heads, DIFF_V_DIM, s), BF16),
            jax.ShapeDtypeStruct((b, ng, 1, s), F32),
            jax.ShapeDtypeStruct((b, ng, 1, s), F32),
        ],
        scratch_shapes=[pltpu.VMEM((d, n), BF16)],
        compiler_params=_cparams(("arbitrary", "arbitrary")),
        name="in_proj",
    )(x3, g, w_all, gq, gk, cpieces)


NEG_BIG = -1e30
SAFE_LOG2_RANGE = 96.0


def _diffattn_kernel(lam_ref, qt_ref, ka_ref, vt_ref, qn2_ref, kn2_ref, gd_ref, *rest,
                     tq, tk, n_sub, n_cast, lam_init, slopes_l2):
    cast_in, o_ref, cast_out = rest[:n_cast], rest[n_cast], rest[n_cast + 1:2 * n_cast + 1]
    qs_ref, corr_ref, m_ref, l_ref, acc_ref, s_refs, p_refs, bmax_refs, kmax_ref = rest[2 * n_cast + 1:]
    _diffattn_body(lam_ref, qt_ref, ka_ref, vt_ref, qn2_ref, kn2_ref, gd_ref, o_ref,
                   qs_ref, corr_ref, m_ref, l_ref, acc_ref, s_refs, p_refs, bmax_refs, kmax_ref,
                   tq=tq, tk=tk, n_sub=n_sub, lam_init=lam_init, slopes_l2=slopes_l2)
    for src, dst in zip(cast_in, cast_out):
        dst[...] = src[...].astype(BF16)


def _diffattn_body(lam_ref, qt_ref, ka_ref, vt_ref, qn2_ref, kn2_ref, gd_ref, o_ref,
                   qs_ref, corr_ref, m_ref, l_ref, acc_ref, s_refs, p_refs, bmax_refs, kmax_ref,
                   *, tq, tk, n_sub, lam_init, slopes_l2):
    h = pl.program_id(1)
    step = pl.program_id(2)
    s_len = ka_ref.shape[1]
    nk = s_len // tk
    ratio = tq // tk
    assert tq % tk == 0 and nk % 2 == 0
    slope = jnp.float32(0.0)
    for hh, sv in enumerate(slopes_l2):
        slope = jnp.where(h == hh, jnp.float32(sv), slope)

    @pl.when(step == 0)
    def _():
        ii = lax.broadcasted_iota(jnp.int32, (tk, tq), 0)
        jj = lax.broadcasted_iota(jnp.int32, (tk, tq), 1)
        corr_ref[0] = jnp.zeros((tk, tq), F32)
        for d in range(ratio):
            corr_ref[1 + d] = (2.0 * slope) * jnp.maximum(ii - jj + d * tk, 0).astype(F32)
        for mp in range(2):
            kmax_ref[mp] = jnp.max(kn2_ref[mp])

    la = lam_ref[...]
    lam = (jnp.exp(jnp.sum(la[0:1] * la[1:2], axis=-1, keepdims=True))
           - jnp.exp(jnp.sum(la[2:3] * la[3:4], axis=-1, keepdims=True)) + lam_init)
    out_gain = gd_ref[...] * (1.0 - lam_init)
    row = lax.broadcasted_iota(jnp.int32, (LANES, 1), 0)
    sign = jnp.where(row < DIFF_HEAD_DIM, 1.0, -1.0).astype(BF16)

    def load_queries(sub):
        for mp in range(2):
            q = qt_ref[mp, :, sub * tq:(sub + 1) * tq]
            qs_ref[sub, 0, mp] = q
            qs_ref[sub, 1, mp] = q * sign

    def finish(sub):
        o1 = acc_ref[sub, 0] * (1.0 / l_ref[sub, 0])
        o2 = acc_ref[sub, 1] * (lam / l_ref[sub, 1])
        da = o1 - o2
        y = da * lax.rsqrt(jnp.mean(da * da, axis=0, keepdims=True) + EPS) * out_gain
        o_ref[sub * tq:(sub + 1) * tq, :] = y.T.astype(BF16)

    def block_start(j):
        return pl.multiple_of(j * tk, tk)

    def weighted_values(j, sub, slot):
        return jnp.dot(vt_ref[:, pl.ds(block_start(j), tk)], p_refs[sub][slot][...],
                       preferred_element_type=F32)

    def add_values(pv, sub, first=False):
        for mp in range(2):
            part = pv[:, mp * tq:(mp + 1) * tq]
            acc_ref[sub, mp] = part if first else acc_ref[sub, mp] + part

    def bounded_path():
        for sub in range(n_sub):
            load_queries(sub)
            first_block = ratio * (n_sub * step + sub)

            def block_of(r):
                j = first_block + r
                return jnp.where(j >= nk, j - nk, j)

            for r in range(nk):
                slot = r % 2
                j = block_of(r)
                if r > 0:
                    pv = weighted_values(block_of(r - 1), sub, 1 - slot)
                above = 0 if r < ratio else (first_block + r < nk).astype(jnp.int32)
                for mp in range(2):
                    k = ka_ref[mp, pl.ds(block_start(j), tk), :]
                    s = jnp.dot(k, qs_ref[sub, above, mp], preferred_element_type=F32)
                    if r < ratio:
                        s = s - corr_ref[1 + r]
                    p = jnp.exp2(s)
                    psum = jnp.sum(p, axis=0, keepdims=True)
                    l_ref[sub, mp] = psum if r == 0 else l_ref[sub, mp] + psum
                    p_refs[sub][slot][:, mp * tq:(mp + 1) * tq] = p.astype(BF16)
                if r > 0:
                    add_values(pv, sub, first=r == 1)
            add_values(weighted_values(block_of(nk - 1), sub, (nk - 1) % 2), sub)
            finish(sub)

    def online_max_path():
        for sub in range(n_sub):
            load_queries(sub)
            qi = n_sub * step + sub
            m_ref[...] = jnp.full(m_ref.shape, NEG_BIG, F32)
            l_ref[sub] = jnp.zeros(l_ref.shape[1:], F32)
            acc_ref[sub] = jnp.zeros(acc_ref.shape[1:], F32)
            p_refs[sub][1][...] = jnp.zeros(p_refs[sub][1].shape, BF16)

            def scores(j, slot):
                d = j - ratio * qi
                above = (d >= ratio).astype(jnp.int32)
                overlap = jnp.where((d >= 0) & (d < ratio), d + 1, 0)
                for mp in range(2):
                    k = ka_ref[mp, pl.ds(block_start(j), tk), :]
                    s = jnp.dot(k, qs_ref[sub, above, mp], preferred_element_type=F32) - corr_ref[overlap]
                    s_refs[slot][mp] = s
                    bmax_refs[slot][mp] = jnp.max(s, axis=0, keepdims=True)

            def region(j, slot):
                pv = weighted_values(jnp.maximum(j - 1, 0), sub, 1 - slot)
                alphas = []
                for mp in range(2):
                    m_old = m_ref[mp]
                    m_new = jnp.maximum(m_old, bmax_refs[slot][mp])
                    p = jnp.exp2(s_refs[slot][mp] - m_new)
                    alpha = jnp.exp2(m_old - m_new)
                    l_ref[sub, mp] = alpha * l_ref[sub, mp] + jnp.sum(p, axis=0, keepdims=True)
                    m_ref[mp] = m_new
                    p_refs[sub][slot][:, mp * tq:(mp + 1) * tq] = p.astype(BF16)
                    alphas.append(alpha)
                scores(jnp.minimum(j + 1, nk - 1), 1 - slot)
                for mp in range(2):
                    acc_ref[sub, mp] = (acc_ref[sub, mp] + pv[:, mp * tq:(mp + 1) * tq]) * alphas[mp]

            scores(0, 0)

            def pair(i, c):
                region(2 * i, 0)
                region(2 * i + 1, 1)
                return c

            lax.fori_loop(0, nk // 2, pair, 0)
            add_values(weighted_values(nk - 1, sub, 1), sub)
            finish(sub)

    bound2 = jnp.float32(0.0)
    for mp in range(2):
        bound2 = jnp.maximum(bound2, jnp.max(qn2_ref[mp]) * kmax_ref[mp])
    lax.cond(bound2 <= SAFE_LOG2_RANGE ** 2, bounded_path, online_max_path)


def _diffattn(lam_params, qt, ka, vt, qn2, kn2, gd_col, weights, layer, tq, tk, n_sub, lam_init, slopes_l2):
    b, ng, _, s = qt.shape
    n_heads = ng // 2
    tqs = n_sub * tq
    nq = s // tqs
    n_steps = b * n_heads * nq
    bf16_rows = 16
    assert all(w.shape[1] % (n_steps * bf16_rows) == 0 for w in weights)
    slab = lambda i, h, q: (i * n_heads + h) * nq + q
    kern = functools.partial(_diffattn_kernel, tq=tq, tk=tk, n_sub=n_sub, n_cast=len(weights),
                             lam_init=lam_init, slopes_l2=slopes_l2)
    outs = pl.pallas_call(
        kern,
        grid=(b, n_heads, nq),
        in_specs=[
            pl.BlockSpec((4, DIFF_HEAD_DIM), lambda i, h, q: (0, 0)),
            pl.BlockSpec((None, 2, LANES, tqs), lambda i, h, q: (i, h, 0, q)),
            pl.BlockSpec((None, 2, s, LANES), lambda i, h, q: (i, h, 0, 0)),
            pl.BlockSpec((None, None, DIFF_V_DIM, s), lambda i, h, q: (i, h, 0, 0)),
            pl.BlockSpec((None, 2, 1, tqs), lambda i, h, q: (i, h, 0, q)),
            pl.BlockSpec((None, 2, 1, s), lambda i, h, q: (i, h, 0, 0)),
            pl.BlockSpec((DIFF_V_DIM, 1), lambda i, h, q: (0, 0)),
        ] + [
            pl.BlockSpec((None, w.shape[1] // n_steps, w.shape[2]), lambda i, h, q: (layer, slab(i, h, q), 0))
            for w in weights
        ],
        out_specs=[pl.BlockSpec((None, tqs, DIFF_V_DIM), lambda i, h, q: (i, q, h))] + [
            pl.BlockSpec((w.shape[1] // n_steps, w.shape[2]), lambda i, h, q: (slab(i, h, q), 0))
            for w in weights
        ],
        out_shape=[jax.ShapeDtypeStruct((b, s, n_heads * DIFF_V_DIM), BF16)] + [
            jax.ShapeDtypeStruct(w.shape[1:], BF16) for w in weights
        ],
        scratch_shapes=[
            pltpu.VMEM((n_sub, 2, 2, LANES, tq), BF16),
            pltpu.VMEM((1 + tq // tk, tk, tq), F32),
            pltpu.VMEM((2, 1, tq), F32),
            pltpu.VMEM((n_sub, 2, 1, tq), F32),
            pltpu.VMEM((n_sub, 2, DIFF_V_DIM, tq), F32),
            [pltpu.VMEM((2, tk, tq), F32)] * 2,
            [[pltpu.VMEM((tk, 2 * tq), BF16)] * 2] * n_sub,
            [pltpu.VMEM((2, 1, tq), F32)] * 2,
            pltpu.SMEM((2,), F32),
        ],
        compiler_params=_cparams(("parallel", "parallel", "arbitrary")),
        name="diffattn",
    )(lam_params, qt, ka, vt, qn2, kn2, gd_col, *weights)
    return outs[0], outs[1:]


def _retention_kernel(dec_ref, q_ref, k_ref, v_ref, g_ref, gn_ref, o_ref,
                      kvf_ref, kvb_ref, rf_ref, rb_ref, a_ref, *, chunk, group):
    j = pl.program_id(1)
    s_len = q_ref.shape[0]
    n_chunks = s_len // chunk
    hd = RET_HEAD_DIM
    c = chunk

    lane = lax.broadcasted_iota(jnp.int32, (1, LANES), 1)
    rowi = lax.broadcasted_iota(jnp.int32, (LANES, 1), 0)
    first_l = lane < hd
    first_r = rowi < hd

    def per_lane(d):
        return -jnp.exp(jnp.where(first_l, dec_ref[d, 2 * j], dec_ref[d, 2 * j + 1]))

    def per_row(d):
        return -jnp.exp(jnp.where(first_r, dec_ref[d, 2 * j], dec_ref[d, 2 * j + 1]))

    lgf_l, lgb_l = per_lane(0), per_lane(1)
    lgf_r, lgb_r = per_row(0), per_row(1)

    pos_r = lax.broadcasted_iota(jnp.int32, (c, 1), 0).astype(F32)
    wk_f = jnp.exp(lgf_l * (c - 1.0 - pos_r))
    wq_f = jnp.exp(lgf_l * (pos_r + 1.0))
    wk_b = jnp.exp(lgb_l * pos_r)
    wq_b = jnp.exp(lgb_l * (c - pos_r))
    dc_f = jnp.exp(lgf_r * float(c))
    dc_b = jnp.exp(lgb_r * float(c))

    tt = lax.broadcasted_iota(jnp.int32, (c, c), 0)
    ss = lax.broadcasted_iota(jnp.int32, (c, c), 1)
    dist = (tt - ss).astype(F32)

    def dmask(hh):
        lf = -jnp.exp(jnp.full((1, 1), dec_ref[0, 2 * j + hh], F32))
        lb = -jnp.exp(jnp.full((1, 1), dec_ref[1, 2 * j + hh], F32))
        return jnp.where(dist >= 0, jnp.exp(lf * jnp.maximum(dist, 0.0)),
                         jnp.exp(lb * jnp.maximum(-dist, 0.0)))

    kscale = RET_HEAD_DIM ** -0.5
    dcat = jnp.concatenate([dmask(0), dmask(1)], axis=1) * kscale
    wk_f = wk_f * kscale
    wk_b = wk_b * kscale
    wq_f = wq_f.astype(BF16)
    wq_b = wq_b.astype(BF16)
    bd = (first_r == first_l).astype(F32).astype(BF16)
    m0f = first_l.astype(F32)
    m0 = m0f.astype(BF16)
    m1 = (1.0 - m0f).astype(BF16)

    tn = (((0,), (0,)), ((), ()))
    nt = (((1,), (1,)), ((), ()))
    assert n_chunks % group == 0

    def chunk_local(i, carry):
        for u in range(group):
            n = i * group + u
            r0 = pl.multiple_of(n * c, c)
            qb = q_ref[pl.ds(r0, c), :]
            kb = k_ref[pl.ds(r0, c), :]
            vb = v_ref[pl.ds(r0, c), :]
            kf = kb.astype(F32)
            kvf_ref[n] = lax.dot_general((kf * wk_f).astype(BF16), vb, tn, preferred_element_type=F32)
            kvb_ref[n] = lax.dot_general((kf * wk_b).astype(BF16), vb, tn, preferred_element_type=F32)
            kcat = jnp.concatenate([kb * m0, kb * m1], axis=0)
            s = lax.dot_general(qb, kcat, nt, preferred_element_type=F32) * dcat
            a_ref[n] = jnp.concatenate([s.astype(BF16), qb * wq_f, qb * wq_b], axis=1)
        return carry

    lax.fori_loop(0, n_chunks // group, chunk_local, 0)

    def scan(n, carry):
        rf, rb = carry
        nb = n_chunks - 1 - n
        rf_ref[n] = rf.astype(BF16)
        rb_ref[nb] = rb.astype(BF16)
        return dc_f * rf + kvf_ref[n], dc_b * rb + kvb_ref[nb]

    zero = jnp.zeros((LANES, LANES), F32)
    lax.fori_loop(0, n_chunks, scan, (zero, zero))

    gn = gn_ref[...]

    def outputs(i, carry):
        for u in range(group):
            n = i * group + u
            r0 = pl.multiple_of(n * c, c)
            vb = v_ref[pl.ds(r0, c), :]
            bm = jnp.concatenate([vb * m0, vb * m1, rf_ref[n] * bd, rb_ref[n] * bd], axis=0)
            o = jnp.dot(a_ref[n], bm, preferred_element_type=F32)
            o2 = o * o
            ms0 = jnp.sum(o2 * m0f, axis=-1, keepdims=True)
            ms1 = jnp.sum(o2 * (1.0 - m0f), axis=-1, keepdims=True)
            ms = jnp.where(first_l, ms0, ms1) * (1.0 / hd)
            y = o * lax.rsqrt(ms + EPS) * gn
            gate = g_ref[pl.ds(r0, c), :].astype(F32)
            y = y * (gate * jax.nn.sigmoid(gate))
            o_ref[pl.ds(r0, c), :] = y.astype(BF16)
        return carry

    lax.fori_loop(0, n_chunks // group, outputs, 0)


def _retention(dec, proj3, gn_lanes, ret_width, chunk, group=8):
    b, s, _ = proj3.shape
    npair = ret_width // LANES
    assert chunk == LANES
    blk = lambda o: pl.BlockSpec((None, s, LANES), lambda i, j, o=o: (i, 0, o * npair + j))
    return pl.pallas_call(
        functools.partial(_retention_kernel, chunk=chunk, group=group),
        grid=(b, npair),
        in_specs=[
            pl.BlockSpec(memory_space=pltpu.SMEM),
            blk(0), blk(1), blk(2), blk(3),
            pl.BlockSpec((1, LANES), lambda i, j: (0, 0)),
        ],
        out_specs=pl.BlockSpec((None, s, LANES), lambda i, j: (i, 0, j)),
        out_shape=jax.ShapeDtypeStruct((b, s, ret_width), BF16),
        scratch_shapes=(
            [pltpu.VMEM((s // chunk, LANES, LANES), F32)] * 2
            + [pltpu.VMEM((s // chunk, LANES, LANES), BF16)] * 2
            + [pltpu.VMEM((s // chunk, chunk, 4 * chunk), BF16)]
        ),
        compiler_params=_cparams(("parallel", "parallel")),
        name="retention",
    )(dec, proj3, proj3, proj3, proj3, gn_lanes)


def _post_kernel(x_ref, r_ref, a_ref, wo_ref, g_ref, wg_ref, wu_ref, wd_ref, o_ref, *, f_chunks):
    mix = jnp.concatenate([r_ref[...], a_ref[...]], axis=1)
    x1 = x_ref[...] + jnp.dot(mix, wo_ref[...], preferred_element_type=F32)
    h = (x1 * lax.rsqrt(jnp.mean(x1 * x1, axis=-1, keepdims=True) + EPS) * g_ref[...]).astype(BF16)
    acc = x1
    for f0, f1 in f_chunks:
        gate = jnp.dot(h, wg_ref[:, f0:f1], preferred_element_type=F32)
        up = jnp.dot(h, wu_ref[:, f0:f1], preferred_element_type=F32)
        act = (gate * jax.nn.sigmoid(gate) * up).astype(BF16)
        acc = acc + jnp.dot(act, wd_ref[f0:f1, :], preferred_element_type=F32)
    o_ref[...] = acc


def _f_chunks(d_ff, mxu_cols=256, max_cols=1536):
    out, f0 = [], 0
    while f0 < d_ff:
        f1 = min(d_ff, f0 + max_cols)
        out.append((f0, f1))
        f0 = f1
    assert all((a % mxu_cols == 0) for a, _ in out)
    return tuple(out)


def _post(x2, ret2, da2, wo, g, wg, wu, wd, tm):
    m, d = x2.shape
    rw, aw = ret2.shape[1], da2.shape[1]
    d_ff = wg.shape[1]
    weight = lambda rows, cols: pl.BlockSpec((rows, cols), lambda i: (0, 0), pipeline_mode=pl.Buffered(1))
    return pl.pallas_call(
        functools.partial(_post_kernel, f_chunks=_f_chunks(d_ff)),
        grid=(m // tm,),
        in_specs=[
            pl.BlockSpec((tm, d), lambda i: (i, 0)),
            pl.BlockSpec((tm, rw), lambda i: (i, 0)),
            pl.BlockSpec((tm, aw), lambda i: (i, 0)),
            weight(rw + aw, d),
            pl.BlockSpec((1, d), lambda i: (0, 0), pipeline_mode=pl.Buffered(1)),
            weight(d, d_ff),
            weight(d, d_ff),
            weight(d_ff, d),
        ],
        out_specs=pl.BlockSpec((tm, d), lambda i: (i, 0)),
        out_shape=jax.ShapeDtypeStruct((m, d), F32),
        compiler_params=_cparams(("parallel",)),
        name="post",
    )(x2, ret2, da2, wo, g, wg, wu, wd)


def _slope_pieces(n_heads):
    slopes = (2.0 ** (-8.0 * np.arange(1, n_heads + 1, dtype=np.float64) / n_heads) * LOG2E).astype(np.float32)
    cols = np.zeros((n_heads, 2, DIFF_HEAD_DIM, 1), np.float32)
    rem = slopes.astype(np.float64)
    for p in range(N_PIECES):
        piece = rem.astype(BF16).astype(np.float64)
        for dgt in range(N_DIGITS):
            cols[:, 0, N_DIGITS * p + dgt, 0] = piece
            cols[:, 1, N_AUG // 2 + N_DIGITS * p + dgt, 0] = piece
        rem = rem - piece
    return tuple(float(v) for v in slopes), jnp.asarray(cols)


def kernel(x, attn_norm_g, w_in, ret_decay_fwd, ret_decay_bwd, ret_norm_g, dq_norm_g, dk_norm_g,
           lambda_q1, lambda_k1, lambda_q2, lambda_k2, diff_norm_g, w_out, ffn_norm_g,
           w_gate, w_up, w_down):
    b, s, d = x.shape
    depth = w_in.shape[0]
    ret_width = d // 2
    diff_width = d - ret_width
    n_dheads = diff_width // DIFF_V_DIM
    tm = 512
    tq = min(1024, s)
    tk = 512
    n_sub = 2 if s % (2 * tq) == 0 else 1
    chunk = 128
    slopes_l2, cpieces = _slope_pieces(n_dheads)

    d_ff = w_gate.shape[2]
    f32_weights = (w_out, w_gate, w_up, w_down.reshape(depth, d, d_ff))

    x2 = x.reshape(b * s, d)
    for l in range(depth):
        lam_init = 0.8 - 0.6 * math.exp(-0.3 * l)
        pr, qt, ka, vt, qn2, kn2 = _in_proj(
            x2.reshape(b, s, d), attn_norm_g[l][None], w_in, l, dq_norm_g[l].astype(F32)[:, None],
            dk_norm_g[l].astype(F32)[:, None], cpieces, n_dheads, 4 * ret_width, tm)

        dec = jnp.stack([ret_decay_fwd[l], ret_decay_bwd[l]]).astype(F32)
        gn_lanes = jnp.tile(ret_norm_g[l].astype(F32), LANES // RET_HEAD_DIM)[None]
        ret = _retention(dec, pr, gn_lanes, ret_width, chunk)

        lam_params = jnp.stack([lambda_q1[l], lambda_k1[l], lambda_q2[l], lambda_k2[l]]).astype(F32)
        da, (wo_b, wg_b, wu_b, wd_b) = _diffattn(
            lam_params, qt, ka, vt, qn2, kn2, diff_norm_g[l].astype(F32)[:, None], f32_weights, l,
            tq, tk, n_sub, lam_init, slopes_l2)

        x2 = _post(x2, ret.reshape(b * s, ret_width), da.reshape(b * s, diff_width),
                   wo_b, ffn_norm_g[l][None], wg_b, wu_b, wd_b.reshape(d_ff, d), tm)
    return x2.reshape(b, s, d)
```

```python
import functools
import math

import numpy as np
import jax
import jax.numpy as jnp
from jax import lax
from jax.experimental import pallas as pl
from jax.experimental.pallas import tpu as pltpu

EPS = 1e-6
LOG2E = 1.4426950408889634

RET_HEAD_DIM = 64
DIFF_HEAD_DIM = 64
DIFF_V_DIM = 128
LANES = 128
VMEM_LIMIT = 56 * 1024 * 1024

BF16 = jnp.bfloat16
F32 = jnp.float32


def _cparams(sem):
    return pltpu.CompilerParams(dimension_semantics=sem, vmem_limit_bytes=VMEM_LIMIT)


N_DIGITS = 3
N_PIECES = 3
N_AUG = 2 * N_DIGITS * N_PIECES
POS_RADIX = 32
CAST_COLS = 512


def _pos_digits(pos):
    d0 = pos % POS_RADIX
    d1 = (pos // POS_RADIX) % POS_RADIX * POS_RADIX
    d2 = pos // (POS_RADIX * POS_RADIX) * (POS_RADIX * POS_RADIX)
    return [d.astype(F32) for d in (d0, d1, d2)]


def _in_proj_kernel(x_ref, g_ref, w_ref, gq_ref, gk_ref, cp_ref,
                    pr_ref, qt_ref, ka_ref, vt_ref, qn2_ref, kn2_ref, wb_ref, *, n_heads, ret_cols):
    ts = x_ref.shape[0]
    ng = 2 * n_heads
    hd = DIFF_HEAD_DIM
    half = N_AUG // 2
    aw = n_heads * DIFF_V_DIM

    @pl.when((pl.program_id(0) == 0) & (pl.program_id(1) == 0))
    def _():
        for c0 in range(0, w_ref.shape[1], CAST_COLS):
            wb_ref[:, c0:c0 + CAST_COLS] = w_ref[:, c0:c0 + CAST_COLS].astype(BF16)

    x = x_ref[...]
    y = (x * lax.rsqrt(jnp.mean(x * x, axis=-1, keepdims=True) + EPS) * g_ref[...]).astype(BF16)
    dqkv = jnp.dot(y, wb_ref[:, ret_cols:], preferred_element_type=F32)
    pr_ref[...] = jnp.dot(y, wb_ref[:, :ret_cols], preferred_element_type=F32).astype(BF16)

    def head_norm_t(xf, g_col, scale):
        xt = xf.T.reshape(ng, hd, ts)
        ms = jnp.mean(xt * xt, axis=1, keepdims=True)
        return (xt * lax.rsqrt(ms + EPS) * (g_col[None] * scale)).astype(BF16)

    qn = head_norm_t(dqkv[:, :aw], gq_ref[...], DIFF_HEAD_DIM ** -0.5 * LOG2E)
    kn = head_norm_t(dqkv[:, aw:2 * aw], gk_ref[...], 1.0)
    qn2_ref[...] = jnp.sum(jnp.square(qn.astype(F32)), axis=1, keepdims=True)
    kn2_ref[...] = jnp.sum(jnp.square(kn.astype(F32)), axis=1, keepdims=True)

    pos = pl.program_id(1) * ts + lax.broadcasted_iota(jnp.int32, (1, ts), 1)
    d0, d1, d2 = _pos_digits(pos)
    r = lax.broadcasted_iota(jnp.int32, (hd, 1), 0)
    which = jnp.where(r < half, r, r - half) % N_DIGITS
    digit = jnp.where(which == 0, d0, jnp.where(which == 1, d1, d2))
    k_digits = jnp.where(r < half, digit, 0.0)
    q_digits = jnp.where((r >= half) & (r < N_AUG), -digit, 0.0)
    for g in range(ng):
        q_rows = (q_digits + cp_ref[g // 2, 0]).astype(BF16)
        k_rows = (k_digits + cp_ref[g // 2, 1]).astype(BF16)
        qt_ref[g] = jnp.concatenate([qn[g], q_rows], axis=0)
        ka_ref[g] = jnp.concatenate([kn[g], k_rows], axis=0).astype(F32).T.astype(BF16)
    vt_ref[...] = dqkv[:, 2 * aw:].T.reshape(n_heads, DIFF_V_DIM, ts).astype(BF16)


def _in_proj(x3, g, w_all, layer, gq, gk, cpieces, n_heads, ret_cols, ts):
    b, s, d = x3.shape
    n = w_all.shape[2]
    ng = 2 * n_heads
    assert n == ret_cols + 3 * n_heads * DIFF_V_DIM and n % CAST_COLS == 0
    const = lambda shape: pl.BlockSpec(shape, lambda i, j: (0,) * len(shape), pipeline_mode=pl.Buffered(1))
    return pl.pallas_call(
        functools.partial(_in_proj_kernel, n_heads=n_heads, ret_cols=ret_cols),
        grid=(b, s // ts),
        in_specs=[
            pl.BlockSpec((None, ts, d), lambda i, j: (i, j, 0)),
            const((1, d)),
            pl.BlockSpec((None, d, n), lambda i, j: (layer, 0, 0), pipeline_mode=pl.Buffered(1)),
            const((DIFF_HEAD_DIM, 1)),
            const((DIFF_HEAD_DIM, 1)),
            const((n_heads, 2, DIFF_HEAD_DIM, 1)),
        ],
        out_specs=[
            pl.BlockSpec((None, ts, ret_cols), lambda i, j: (i, j, 0)),
            pl.BlockSpec((None, ng, LANES, ts), lambda i, j: (i, 0, 0, j)),
            pl.BlockSpec((None, ng, ts, LANES), lambda i, j: (i, 0, j, 0)),
            pl.BlockSpec((None, n_heads, DIFF_V_DIM, ts), lambda i, j: (i, 0, 0, j)),
            pl.BlockSpec((None, ng, 1, ts), lambda i, j: (i, 0, 0, j)),
            pl.BlockSpec((None, ng, 1, ts), lambda i, j: (i, 0, 0, j)),
        ],
        out_shape=[
            jax.ShapeDtypeStruct((b, s, ret_cols), BF16),
            jax.ShapeDtypeStruct((b, ng, LANES, s), BF16),
            jax.ShapeDtypeStruct((b, ng, s, LANES), BF16),
            jax.ShapeDtypeStruct((b, n_heads, DIFF_V_DIM, s), BF16),
            jax.ShapeDtypeStruct((b, ng, 1, s), F32),
            jax.ShapeDtypeStruct((b, ng, 1, s), F32),
        ],
        scratch_shapes=[pltpu.VMEM((d, n), BF16)],
        compiler_params=_cparams(("arbitrary", "arbitrary")),
        name="in_proj",
    )(x3, g, w_all, gq, gk, cpieces)


NEG_BIG = -1e30
SAFE_LOG2_RANGE = 96.0


def _diffattn_kernel(lam_ref, qt_ref, ka_ref, vt_ref, qn2_ref, kn2_ref, gd_ref, *rest,
                     tq, tk, n_sub, n_cast, lam_init, slopes_l2):
    cast_in, o_ref, cast_out = rest[:n_cast], rest[n_cast], rest[n_cast + 1:2 * n_cast + 1]
    qs_ref, corr_ref, m_ref, l_ref, acc_ref, s_refs, p_refs, bmax_refs, kmax_ref = rest[2 * n_cast + 1:]
    _diffattn_body(lam_ref, qt_ref, ka_ref, vt_ref, qn2_ref, kn2_ref, gd_ref, o_ref,
                   qs_ref, corr_ref, m_ref, l_ref, acc_ref, s_refs, p_refs, bmax_refs, kmax_ref,
                   tq=tq, tk=tk, n_sub=n_sub, lam_init=lam_init, slopes_l2=slopes_l2)
    for src, dst in zip(cast_in, cast_out):
        dst[...] = src[...].astype(BF16)


def _diffattn_body(lam_ref, qt_ref, ka_ref, vt_ref, qn2_ref, kn2_ref, gd_ref, o_ref,
                   qs_ref, corr_ref, m_ref, l_ref, acc_ref, s_refs, p_refs, bmax_refs, kmax_ref,
                   *, tq, tk, n_sub, lam_init, slopes_l2):
    h = pl.program_id(1)
    step = pl.program_id(2)
    s_len = ka_ref.shape[1]
    nk = s_len // tk
    ratio = tq // tk
    assert tq % tk == 0 and nk % 2 == 0
    slope = jnp.float32(0.0)
    for hh, sv in enumerate(slopes_l2):
        slope = jnp.where(h == hh, jnp.float32(sv), slope)

    @pl.when(step == 0)
    def _():
        ii = lax.broadcasted_iota(jnp.int32, (tk, tq), 0)
        jj = lax.broadcasted_iota(jnp.int32, (tk, tq), 1)
        corr_ref[0] = jnp.zeros((tk, tq), F32)
        for d in range(ratio):
            corr_ref[1 + d] = (2.0 * slope) * jnp.maximum(ii - jj + d * tk, 0).astype(F32)
        for mp in range(2):
            kmax_ref[mp] = jnp.max(kn2_ref[mp])

    la = lam_ref[...]
    lam = (jnp.exp(jnp.sum(la[0:1] * la[1:2], axis=-1, keepdims=True))
           - jnp.exp(jnp.sum(la[2:3] * la[3:4], axis=-1, keepdims=True)) + lam_init)
    out_gain = gd_ref[...] * (1.0 - lam_init)
    row = lax.broadcasted_iota(jnp.int32, (LANES, 1), 0)
    sign = jnp.where(row < DIFF_HEAD_DIM, 1.0, -1.0).astype(BF16)

    def load_queries(sub):
        for mp in range(2):
            q = qt_ref[mp, :, sub * tq:(sub + 1) * tq]
            qs_ref[sub, 0, mp] = q
            qs_ref[sub, 1, mp] = q * sign

    def finish(sub):
        o1 = acc_ref[sub, 0] * (1.0 / l_ref[sub, 0])
        o2 = acc_ref[sub, 1] * (lam / l_ref[sub, 1])
        da = o1 - o2
        y = da * lax.rsqrt(jnp.mean(da * da, axis=0, keepdims=True) + EPS) * out_gain
        o_ref[sub * tq:(sub + 1) * tq, :] = y.T.astype(BF16)

    def block_start(j):
        return pl.multiple_of(j * tk, tk)

    def weighted_values(j, sub, slot):
        return jnp.dot(vt_ref[:, pl.ds(block_start(j), tk)], p_refs[sub][slot][...],
                       preferred_element_type=F32)

    def add_values(pv, sub, first=False):
        for mp in range(2):
            part = pv[:, mp * tq:(mp + 1) * tq]
            acc_ref[sub, mp] = part if first else acc_ref[sub, mp] + part

    def bounded_path():
        for sub in range(n_sub):
            load_queries(sub)
            first_block = ratio * (n_sub * step + sub)

            def block_of(r):
                j = first_block + r
                return jnp.where(j >= nk, j - nk, j)

            for r in range(nk):
                slot = r % 2
                j = block_of(r)
                if r > 0:
                    pv = weighted_values(block_of(r - 1), sub, 1 - slot)
                above = 0 if r < ratio else (first_block + r < nk).astype(jnp.int32)
                for mp in range(2):
                    k = ka_ref[mp, pl.ds(block_start(j), tk), :]
                    s = jnp.dot(k, qs_ref[sub, above, mp], preferred_element_type=F32)
                    if r < ratio:
                        s = s - corr_ref[1 + r]
                    p = jnp.exp2(s)
                    psum = jnp.sum(p, axis=0, keepdims=True)
                    l_ref[sub, mp] = psum if r == 0 else l_ref[sub, mp] + psum
                    p_refs[sub][slot][:, mp * tq:(mp + 1) * tq] = p.astype(BF16)
                if r > 0:
                    add_values(pv, sub, first=r == 1)
            add_values(weighted_values(block_of(nk - 1), sub, (nk - 1) % 2), sub)
            finish(sub)

    def online_max_path():
        for sub in range(n_sub):
            load_queries(sub)
            qi = n_sub * step + sub
            m_ref[...] = jnp.full(m_ref.shape, NEG_BIG, F32)
            l_ref[sub] = jnp.zeros(l_ref.shape[1:], F32)
            acc_ref[sub] = jnp.zeros(acc_ref.shape[1:], F32)
            p_refs[sub][1][...] = jnp.zeros(p_refs[sub][1].shape, BF16)

            def scores(j, slot):
                d = j - ratio * qi
                above = (d >= ratio).astype(jnp.int32)
                overlap = jnp.where((d >= 0) & (d < ratio), d + 1, 0)
                for mp in range(2):
                    k = ka_ref[mp, pl.ds(block_start(j), tk), :]
                    s = jnp.dot(k, qs_ref[sub, above, mp], preferred_element_type=F32) - corr_ref[overlap]
                    s_refs[slot][mp] = s
                    bmax_refs[slot][mp] = jnp.max(s, axis=0, keepdims=True)

            def region(j, slot):
                pv = weighted_values(jnp.maximum(j - 1, 0), sub, 1 - slot)
                alphas = []
                for mp in range(2):
                    m_old = m_ref[mp]
                    m_new = jnp.maximum(m_old, bmax_refs[slot][mp])
                    p = jnp.exp2(s_refs[slot][mp] - m_new)
                    alpha = jnp.exp2(m_old - m_new)
                    l_ref[sub, mp] = alpha * l_ref[sub, mp] + jnp.sum(p, axis=0, keepdims=True)
                    m_ref[mp] = m_new
                    p_refs[sub][slot][:, mp * tq:(mp + 1) * tq] = p.astype(BF16)
                    alphas.append(alpha)
                scores(jnp.minimum(j + 1, nk - 1), 1 - slot)
                for mp in range(2):
                    acc_ref[sub, mp] = (acc_ref[sub, mp] + pv[:, mp * tq:(mp + 1) * tq]) * alphas[mp]

            scores(0, 0)

            def pair(i, c):
                region(2 * i, 0)
                region(2 * i + 1, 1)
                return c

            lax.fori_loop(0, nk // 2, pair, 0)
            add_values(weighted_values(nk - 1, sub, 1), sub)
            finish(sub)

    bound2 = jnp.float32(0.0)
    for mp in range(2):
        bound2 = jnp.maximum(bound2, jnp.max(qn2_ref[mp]) * kmax_ref[mp])
    lax.cond(bound2 <= SAFE_LOG2_RANGE ** 2, bounded_path, online_max_path)


def _diffattn(lam_params, qt, ka, vt, qn2, kn2, gd_col, weights, layer, tq, tk, n_sub, lam_init, slopes_l2):
    b, ng, _, s = qt.shape
    n_heads = ng // 2
    tqs = n_sub * tq
    nq = s // tqs
    n_steps = b * n_heads * nq
    bf16_rows = 16

    def slab_rows(rows):
        for k in range(1, rows // bf16_rows + 1):
            if rows % (k * bf16_rows) == 0 and rows // (k * bf16_rows) <= n_steps:
                return k * bf16_rows
        raise ValueError("weight rows do not split into bf16 slabs")

    slabs = [slab_rows(w.shape[1]) for w in weights]

    def slab_index(rows, rps):
        last = rows // rps - 1
        return lambda i, h, q: jnp.minimum((i * n_heads + h) * nq + q, last)

    kern = functools.partial(_diffattn_kernel, tq=tq, tk=tk, n_sub=n_sub, n_cast=len(weights),
                             lam_init=lam_init, slopes_l2=slopes_l2)
    outs = pl.pallas_call(
        kern,
        grid=(b, n_heads, nq),
        in_specs=[
            pl.BlockSpec((4, DIFF_HEAD_DIM), lambda i, h, q: (0, 0)),
            pl.BlockSpec((None, 2, LANES, tqs), lambda i, h, q: (i, h, 0, q)),
            pl.BlockSpec((None, 2, s, LANES), lambda i, h, q: (i, h, 0, 0)),
            pl.BlockSpec((None, None, DIFF_V_DIM, s), lambda i, h, q: (i, h, 0, 0)),
            pl.BlockSpec((None, 2, 1, tqs), lambda i, h, q: (i, h, 0, q)),
            pl.BlockSpec((None, 2, 1, s), lambda i, h, q: (i, h, 0, 0)),
            pl.BlockSpec((DIFF_V_DIM, 1), lambda i, h, q: (0, 0)),
        ] + [
            pl.BlockSpec((None, rps, w.shape[2]),
                         lambda i, h, q, f=slab_index(w.shape[1], rps): (layer, f(i, h, q), 0))
            for w, rps in zip(weights, slabs)
        ],
        out_specs=[pl.BlockSpec((None, tqs, DIFF_V_DIM), lambda i, h, q: (i, q, h))] + [
            pl.BlockSpec((rps, w.shape[2]), lambda i, h, q, f=slab_index(w.shape[1], rps): (f(i, h, q), 0))
            for w, rps in zip(weights, slabs)
        ],
        out_shape=[jax.ShapeDtypeStruct((b, s, n_heads * DIFF_V_DIM), BF16)] + [
            jax.ShapeDtypeStruct(w.shape[1:], BF16) for w in weights
        ],
        scratch_shapes=[
            pltpu.VMEM((n_sub, 2, 2, LANES, tq), BF16),
            pltpu.VMEM((1 + tq // tk, tk, tq), F32),
            pltpu.VMEM((2, 1, tq), F32),
            pltpu.VMEM((n_sub, 2, 1, tq), F32),
            pltpu.VMEM((n_sub, 2, DIFF_V_DIM, tq), F32),
            [pltpu.VMEM((2, tk, tq), F32)] * 2,
            [[pltpu.VMEM((tk, 2 * tq), BF16)] * 2] * n_sub,
            [pltpu.VMEM((2, 1, tq), F32)] * 2,
            pltpu.SMEM((2,), F32),
        ],
        compiler_params=_cparams(("arbitrary", "arbitrary", "arbitrary")),
        name="diffattn",
    )(lam_params, qt, ka, vt, qn2, kn2, gd_col, *weights)
    return outs[0], outs[1:]


def _retention_kernel(dec_ref, q_ref, k_ref, v_ref, g_ref, gn_ref, o_ref,
                      kvf_ref, kvb_ref, rf_ref, rb_ref, a_ref, *, chunk, group):
    j = pl.program_id(1)
    s_len = q_ref.shape[0]
    n_chunks = s_len // chunk
    hd = RET_HEAD_DIM
    c = chunk

    lane = lax.broadcasted_iota(jnp.int32, (1, LANES), 1)
    rowi = lax.broadcasted_iota(jnp.int32, (LANES, 1), 0)
    first_l = lane < hd
    first_r = rowi < hd

    def per_lane(d):
        return -jnp.exp(jnp.where(first_l, dec_ref[d, 2 * j], dec_ref[d, 2 * j + 1]))

    def per_row(d):
        return -jnp.exp(jnp.where(first_r, dec_ref[d, 2 * j], dec_ref[d, 2 * j + 1]))

    lgf_l, lgb_l = per_lane(0), per_lane(1)
    lgf_r, lgb_r = per_row(0), per_row(1)

    pos_r = lax.broadcasted_iota(jnp.int32, (c, 1), 0).astype(F32)
    wk_f = jnp.exp(lgf_l * (c - 1.0 - pos_r))
    wq_f = jnp.exp(lgf_l * (pos_r + 1.0))
    wk_b = jnp.exp(lgb_l * pos_r)
    wq_b = jnp.exp(lgb_l * (c - pos_r))
    dc_f = jnp.exp(lgf_r * float(c))
    dc_b = jnp.exp(lgb_r * float(c))

    tt = lax.broadcasted_iota(jnp.int32, (c, c), 0)
    ss = lax.broadcasted_iota(jnp.int32, (c, c), 1)
    dist = (tt - ss).astype(F32)

    def dmask(hh):
        lf = -jnp.exp(jnp.full((1, 1), dec_ref[0, 2 * j + hh], F32))
        lb = -jnp.exp(jnp.full((1, 1), dec_ref[1, 2 * j + hh], F32))
        return jnp.where(dist >= 0, jnp.exp(lf * jnp.maximum(dist, 0.0)),
                         jnp.exp(lb * jnp.maximum(-dist, 0.0)))

    kscale = RET_HEAD_DIM ** -0.5
    dcat = jnp.concatenate([dmask(0), dmask(1)], axis=1) * kscale
    wk_f = wk_f * kscale
    wk_b = wk_b * kscale
    wq_f = wq_f.astype(BF16)
    wq_b = wq_b.astype(BF16)
    bd = (first_r == first_l).astype(F32).astype(BF16)
    m0f = first_l.astype(F32)
    m0 = m0f.astype(BF16)
    m1 = (1.0 - m0f).astype(BF16)

    tn = (((0,), (0,)), ((), ()))
    nt = (((1,), (1,)), ((), ()))
    assert n_chunks % group == 0

    def chunk_local(i, carry):
        for u in range(group):
            n = i * group + u
            r0 = pl.multiple_of(n * c, c)
            qb = q_ref[pl.ds(r0, c), :]
            kb = k_ref[pl.ds(r0, c), :]
            vb = v_ref[pl.ds(r0, c), :]
            kf = kb.astype(F32)
            kvf_ref[n] = lax.dot_general((kf * wk_f).astype(BF16), vb, tn, preferred_element_type=F32)
            kvb_ref[n] = lax.dot_general((kf * wk_b).astype(BF16), vb, tn, preferred_element_type=F32)
            kcat = jnp.concatenate([kb * m0, kb * m1], axis=0)
            s = lax.dot_general(qb, kcat, nt, preferred_element_type=F32) * dcat
            a_ref[n] = jnp.concatenate([s.astype(BF16), qb * wq_f, qb * wq_b], axis=1)
        return carry

    lax.fori_loop(0, n_chunks // group, chunk_local, 0)

    def scan(n, carry):
        rf, rb = carry
        nb = n_chunks - 1 - n
        rf_ref[n] = rf.astype(BF16)
        rb_ref[nb] = rb.astype(BF16)
        return dc_f * rf + kvf_ref[n], dc_b * rb + kvb_ref[nb]

    zero = jnp.zeros((LANES, LANES), F32)
    lax.fori_loop(0, n_chunks, scan, (zero, zero))

    gn = gn_ref[...]

    def outputs(i, carry):
        for u in range(group):
            n = i * group + u
            r0 = pl.multiple_of(n * c, c)
            vb = v_ref[pl.ds(r0, c), :]
            bm = jnp.concatenate([vb * m0, vb * m1, rf_ref[n] * bd, rb_ref[n] * bd], axis=0)
            o = jnp.dot(a_ref[n], bm, preferred_element_type=F32)
            o2 = o * o
            ms0 = jnp.sum(o2 * m0f, axis=-1, keepdims=True)
            ms1 = jnp.sum(o2 * (1.0 - m0f), axis=-1, keepdims=True)
            ms = jnp.where(first_l, ms0, ms1) * (1.0 / hd)
            y = o * lax.rsqrt(ms + EPS) * gn
            gate = g_ref[pl.ds(r0, c), :].astype(F32)
            y = y * (gate * jax.nn.sigmoid(gate))
            o_ref[pl.ds(r0, c), :] = y.astype(BF16)
        return carry

    lax.fori_loop(0, n_chunks // group, outputs, 0)


def _retention(dec, proj3, gn_lanes, ret_width, chunk, group=8):
    b, s, _ = proj3.shape
    npair = ret_width // LANES
    assert chunk == LANES
    blk = lambda o: pl.BlockSpec((None, s, LANES), lambda i, j, o=o: (i, 0, o * npair + j))
    return pl.pallas_call(
        functools.partial(_retention_kernel, chunk=chunk, group=group),
        grid=(b, npair),
        in_specs=[
            pl.BlockSpec(memory_space=pltpu.SMEM),
            blk(0), blk(1), blk(2), blk(3),
            pl.BlockSpec((1, LANES), lambda i, j: (0, 0)),
        ],
        out_specs=pl.BlockSpec((None, s, LANES), lambda i, j: (i, 0, j)),
        out_shape=jax.ShapeDtypeStruct((b, s, ret_width), BF16),
        scratch_shapes=(
            [pltpu.VMEM((s // chunk, LANES, LANES), F32)] * 2
            + [pltpu.VMEM((s // chunk, LANES, LANES), BF16)] * 2
            + [pltpu.VMEM((s // chunk, chunk, 4 * chunk), BF16)]
        ),
        compiler_params=_cparams(("parallel", "parallel")),
        name="retention",
    )(dec, proj3, proj3, proj3, proj3, gn_lanes)


def _post_kernel(x_ref, r_ref, a_ref, wo_ref, g_ref, wg_ref, wu_ref, wd_ref, o_ref, *, f_chunks):
    mix = jnp.concatenate([r_ref[...], a_ref[...]], axis=1)
    x1 = x_ref[...] + jnp.dot(mix, wo_ref[...], preferred_element_type=F32)
    h = (x1 * lax.rsqrt(jnp.mean(x1 * x1, axis=-1, keepdims=True) + EPS) * g_ref[...]).astype(BF16)
    acc = x1
    for f0, f1 in f_chunks:
        gate = jnp.dot(h, wg_ref[:, f0:f1], preferred_element_type=F32)
        up = jnp.dot(h, wu_ref[:, f0:f1], preferred_element_type=F32)
        act = (gate * jax.nn.sigmoid(gate) * up).astype(BF16)
        acc = acc + jnp.dot(act, wd_ref[f0:f1, :], preferred_element_type=F32)
    o_ref[...] = acc


def _f_chunks(d_ff, mxu_cols=256, max_cols=1536):
    out, f0 = [], 0
    while f0 < d_ff:
        f1 = min(d_ff, f0 + max_cols)
        out.append((f0, f1))
        f0 = f1
    assert all((a % mxu_cols == 0) for a, _ in out)
    return tuple(out)


def _post(x2, ret2, da2, wo, g, wg, wu, wd, tm):
    m, d = x2.shape
    rw, aw = ret2.shape[1], da2.shape[1]
    d_ff = wg.shape[1]
    weight = lambda rows, cols: pl.BlockSpec((rows, cols), lambda i: (0, 0), pipeline_mode=pl.Buffered(1))
    return pl.pallas_call(
        functools.partial(_post_kernel, f_chunks=_f_chunks(d_ff)),
        grid=(m // tm,),
        in_specs=[
            pl.BlockSpec((tm, d), lambda i: (i, 0)),
            pl.BlockSpec((tm, rw), lambda i: (i, 0)),
            pl.BlockSpec((tm, aw), lambda i: (i, 0)),
            weight(rw + aw, d),
            pl.BlockSpec((1, d), lambda i: (0, 0), pipeline_mode=pl.Buffered(1)),
            weight(d, d_ff),
            weight(d, d_ff),
            weight(d_ff, d),
        ],
        out_specs=pl.BlockSpec((tm, d), lambda i: (i, 0)),
        out_shape=jax.ShapeDtypeStruct((m, d), F32),
        compiler_params=_cparams(("parallel",)),
        name="post",
    )(x2, ret2, da2, wo, g, wg, wu, wd)


def _slope_pieces(n_heads):
    slopes = (2.0 ** (-8.0 * np.arange(1, n_heads + 1, dtype=np.float64) / n_heads) * LOG2E).astype(np.float32)
    cols = np.zeros((n_heads, 2, DIFF_HEAD_DIM, 1), np.float32)
    rem = slopes.astype(np.float64)
    for p in range(N_PIECES):
        piece = rem.astype(BF16).astype(np.float64)
        for dgt in range(N_DIGITS):
            cols[:, 0, N_DIGITS * p + dgt, 0] = piece
            cols[:, 1, N_AUG // 2 + N_DIGITS * p + dgt, 0] = piece
        rem = rem - piece
    return tuple(float(v) for v in slopes), jnp.asarray(cols)


def kernel(x, attn_norm_g, w_in, ret_decay_fwd, ret_decay_bwd, ret_norm_g, dq_norm_g, dk_norm_g,
           lambda_q1, lambda_k1, lambda_q2, lambda_k2, diff_norm_g, w_out, ffn_norm_g,
           w_gate, w_up, w_down):
    b, s, d = x.shape
    depth = w_in.shape[0]
    ret_width = d // 2
    diff_width = d - ret_width
    n_dheads = diff_width // DIFF_V_DIM
    tm = 512
    tq = min(1024, s)
    tk = 512
    n_sub = 2 if s % (2 * tq) == 0 else 1
    chunk = 128
    slopes_l2, cpieces = _slope_pieces(n_dheads)

    f32_weights = (w_out, w_gate, w_up, w_down)

    x2 = x.reshape(b * s, d)
    for l in range(depth):
        lam_init = 0.8 - 0.6 * math.exp(-0.3 * l)
        pr, qt, ka, vt, qn2, kn2 = _in_proj(
            x2.reshape(b, s, d), attn_norm_g[l][None], w_in, l, dq_norm_g[l].astype(F32)[:, None],
            dk_norm_g[l].astype(F32)[:, None], cpieces, n_dheads, 4 * ret_width, tm)

        dec = jnp.stack([ret_decay_fwd[l], ret_decay_bwd[l]]).astype(F32)
        gn_lanes = jnp.tile(ret_norm_g[l].astype(F32), LANES // RET_HEAD_DIM)[None]
        ret = _retention(dec, pr, gn_lanes, ret_width, chunk)

        lam_params = jnp.stack([lambda_q1[l], lambda_k1[l], lambda_q2[l], lambda_k2[l]]).astype(F32)
        da, (wo_b, wg_b, wu_b, wd_b) = _diffattn(
            lam_params, qt, ka, vt, qn2, kn2, diff_norm_g[l].astype(F32)[:, None], f32_weights, l,
            tq, tk, n_sub, lam_init, slopes_l2)

        x2 = _post(x2, ret.reshape(b * s, ret_width), da.reshape(b * s, diff_width),
                   wo_b, ffn_norm_g[l][None], wg_b, wu_b, wd_b, tm)
    return x2.reshape(b, s, d)
```

```python
import functools
import math

import numpy as np
import jax
import jax.numpy as jnp
from jax import lax
from jax.experimental import pallas as pl
from jax.experimental.pallas import tpu as pltpu

EPS = 1e-6
LOG2E = 1.4426950408889634

RET_HEAD_DIM = 64
DIFF_HEAD_DIM = 64
DIFF_V_DIM = 128
LANES = 128
VMEM_LIMIT = 56 * 1024 * 1024

BF16 = jnp.bfloat16
F32 = jnp.float32


def _cparams(sem):
    return pltpu.CompilerParams(dimension_semantics=sem, vmem_limit_bytes=VMEM_LIMIT)


N_DIGITS = 3
N_PIECES = 3
N_AUG = 2 * N_DIGITS * N_PIECES
POS_RADIX = 32
CAST_COLS = 512


def _pos_digits(pos):
    d0 = pos % POS_RADIX
    d1 = (pos // POS_RADIX) % POS_RADIX * POS_RADIX
    d2 = pos // (POS_RADIX * POS_RADIX) * (POS_RADIX * POS_RADIX)
    return [d.astype(F32) for d in (d0, d1, d2)]


def _in_proj_kernel(x_ref, g_ref, w_ref, gq_ref, gk_ref, cp_ref,
                    pr_ref, qt_ref, ka_ref, vt_ref, qn2_ref, kn2_ref, wb_ref, *, n_heads, ret_cols):
    ts = x_ref.shape[0]
    ng = 2 * n_heads
    hd = DIFF_HEAD_DIM
    half = N_AUG // 2
    aw = n_heads * DIFF_V_DIM

    @pl.when((pl.program_id(0) == 0) & (pl.program_id(1) == 0))
    def _():
        for c0 in range(0, w_ref.shape[1], CAST_COLS):
            wb_ref[:, c0:c0 + CAST_COLS] = w_ref[:, c0:c0 + CAST_COLS].astype(BF16)

    x = x_ref[...]
    y = (x * lax.rsqrt(jnp.mean(x * x, axis=-1, keepdims=True) + EPS) * g_ref[...]).astype(BF16)
    dqkv = jnp.dot(y, wb_ref[:, ret_cols:], preferred_element_type=F32)
    pr_ref[...] = jnp.dot(y, wb_ref[:, :ret_cols], preferred_element_type=F32).astype(BF16)

    def head_norm_t(xf, g_col, scale):
        xt = xf.T.reshape(ng, hd, ts)
        ms = jnp.mean(xt * xt, axis=1, keepdims=True)
        return (xt * lax.rsqrt(ms + EPS) * (g_col[None] * scale)).astype(BF16)

    qn = head_norm_t(dqkv[:, :aw], gq_ref[...], DIFF_HEAD_DIM ** -0.5 * LOG2E)
    kn = head_norm_t(dqkv[:, aw:2 * aw], gk_ref[...], 1.0)
    qn2_ref[...] = jnp.sum(jnp.square(qn.astype(F32)), axis=1, keepdims=True)
    kn2_ref[...] = jnp.sum(jnp.square(kn.astype(F32)), axis=1, keepdims=True)

    pos = pl.program_id(1) * ts + lax.broadcasted_iota(jnp.int32, (1, ts), 1)
    d0, d1, d2 = _pos_digits(pos)
    r = lax.broadcasted_iota(jnp.int32, (hd, 1), 0)
    which = jnp.where(r < half, r, r - half) % N_DIGITS
    digit = jnp.where(which == 0, d0, jnp.where(which == 1, d1, d2))
    k_digits = jnp.where(r < half, digit, 0.0)
    q_digits = jnp.where((r >= half) & (r < N_AUG), -digit, 0.0)
    for g in range(ng):
        q_rows = (q_digits + cp_ref[g // 2, 0]).astype(BF16)
        k_rows = (k_digits + cp_ref[g // 2, 1]).astype(BF16)
        qt_ref[g] = jnp.concatenate([qn[g], q_rows], axis=0)
        ka_ref[g] = jnp.concatenate([kn[g], k_rows], axis=0).astype(F32).T.astype(BF16)
    vt_ref[...] = dqkv[:, 2 * aw:].T.reshape(n_heads, DIFF_V_DIM, ts).astype(BF16)


def _in_proj(x3, g, w_all, layer, gq, gk, cpieces, n_heads, ret_cols, ts):
    b, s, d = x3.shape
    n = w_all.shape[2]
    ng = 2 * n_heads
    assert n == ret_cols + 3 * n_heads * DIFF_V_DIM and n % CAST_COLS == 0
    const = lambda shape: pl.BlockSpec(shape, lambda i, j: (0,) * len(shape), pipeline_mode=pl.Buffered(1))
    return pl.pallas_call(
        functools.partial(_in_proj_kernel, n_heads=n_heads, ret_cols=ret_cols),
        grid=(b, s // ts),
        in_specs=[
            pl.BlockSpec((None, ts, d), lambda i, j: (i, j, 0)),
            const((1, d)),
            pl.BlockSpec((None, d, n), lambda i, j: (layer, 0, 0), pipeline_mode=pl.Buffered(1)),
            const((DIFF_HEAD_DIM, 1)),
            const((DIFF_HEAD_DIM, 1)),
            const((n_heads, 2, DIFF_HEAD_DIM, 1)),
        ],
        out_specs=[
            pl.BlockSpec((None, ts, ret_cols), lambda i, j: (i, j, 0)),
            pl.BlockSpec((None, ng, LANES, ts), lambda i, j: (i, 0, 0, j)),
            pl.BlockSpec((None, ng, ts, LANES), lambda i, j: (i, 0, j, 0)),
            pl.BlockSpec((None, n_heads, DIFF_V_DIM, ts), lambda i, j: (i, 0, 0, j)),
            pl.BlockSpec((None, ng, 1, ts), lambda i, j: (i, 0, 0, j)),
            pl.BlockSpec((None, ng, 1, ts), lambda i, j: (i, 0, 0, j)),
        ],
        out_shape=[
            jax.ShapeDtypeStruct((b, s, ret_cols), BF16),
            jax.ShapeDtypeStruct((b, ng, LANES, s), BF16),
            jax.ShapeDtypeStruct((b, ng, s, LANES), BF16),
            jax.ShapeDtypeStruct((b, n_heads, DIFF_V_DIM, s), BF16),
            jax.ShapeDtypeStruct((b, ng, 1, s), F32),
            jax.ShapeDtypeStruct((b, ng, 1, s), F32),
        ],
        scratch_shapes=[pltpu.VMEM((d, n), BF16)],
        compiler_params=_cparams(("arbitrary", "arbitrary")),
        name="in_proj",
    )(x3, g, w_all, gq, gk, cpieces)


NEG_BIG = -1e30
SAFE_LOG2_RANGE = 96.0
N_BUF = 2


def _diffattn_kernel(lam_ref, qt_ref, ka_ref, vt_ref, qn2_ref, kn2_ref, gd_ref, *rest,
                     tq, tk, n_sub, n_cast, lam_init, slopes_l2):
    cast_in, o_ref, cast_out = rest[:n_cast], rest[n_cast], rest[n_cast + 1:2 * n_cast + 1]
    qs_ref, corr_ref, m_ref, l_ref, acc_ref, s_refs, p_refs, bmax_refs, kmax_ref = rest[2 * n_cast + 1:]
    _diffattn_body(lam_ref, qt_ref, ka_ref, vt_ref, qn2_ref, kn2_ref, gd_ref, o_ref,
                   qs_ref, corr_ref, m_ref, l_ref, acc_ref, s_refs, p_refs, bmax_refs, kmax_ref,
                   tq=tq, tk=tk, n_sub=n_sub, lam_init=lam_init, slopes_l2=slopes_l2)
    for src, dst in zip(cast_in, cast_out):
        dst[...] = src[...].astype(BF16)


def _diffattn_body(lam_ref, qt_ref, ka_ref, vt_ref, qn2_ref, kn2_ref, gd_ref, o_ref,
                   qs_ref, corr_ref, m_ref, l_ref, acc_ref, s_refs, p_refs, bmax_refs, kmax_ref,
                   *, tq, tk, n_sub, lam_init, slopes_l2):
    h = pl.program_id(1)
    step = pl.program_id(2)
    s_len = ka_ref.shape[1]
    nk = s_len // tk
    ratio = tq // tk
    assert tq % tk == 0 and nk % 2 == 0
    slope = jnp.float32(0.0)
    for hh, sv in enumerate(slopes_l2):
        slope = jnp.where(h == hh, jnp.float32(sv), slope)

    @pl.when(step == 0)
    def _():
        ii = lax.broadcasted_iota(jnp.int32, (tk, tq), 0)
        jj = lax.broadcasted_iota(jnp.int32, (tk, tq), 1)
        corr_ref[0] = jnp.zeros((tk, tq), F32)
        for d in range(ratio):
            corr_ref[1 + d] = (2.0 * slope) * jnp.maximum(ii - jj + d * tk, 0).astype(F32)
        for mp in range(2):
            kmax_ref[mp] = jnp.max(kn2_ref[mp])

    la = lam_ref[...]
    lam = (jnp.exp(jnp.sum(la[0:1] * la[1:2], axis=-1, keepdims=True))
           - jnp.exp(jnp.sum(la[2:3] * la[3:4], axis=-1, keepdims=True)) + lam_init)
    out_gain = gd_ref[...] * (1.0 - lam_init)
    row = lax.broadcasted_iota(jnp.int32, (LANES, 1), 0)
    sign = jnp.where(row < DIFF_HEAD_DIM, 1.0, -1.0).astype(BF16)

    def load_queries(sub):
        for mp in range(2):
            q = qt_ref[mp, :, sub * tq:(sub + 1) * tq]
            qs_ref[sub % N_BUF, 0, mp] = q
            qs_ref[sub % N_BUF, 1, mp] = q * sign

    def finish(sub):
        o1 = acc_ref[sub % N_BUF, 0] * (1.0 / l_ref[sub % N_BUF, 0])
        o2 = acc_ref[sub % N_BUF, 1] * (lam / l_ref[sub % N_BUF, 1])
        da = o1 - o2
        y = da * lax.rsqrt(jnp.mean(da * da, axis=0, keepdims=True) + EPS) * out_gain
        o_ref[sub * tq:(sub + 1) * tq, :] = y.T.astype(BF16)

    def block_start(j):
        return pl.multiple_of(j * tk, tk)

    def weighted_values(j, sub, slot):
        return jnp.dot(vt_ref[:, pl.ds(block_start(j), tk)], p_refs[sub % N_BUF][slot][...],
                       preferred_element_type=F32)

    def add_values(pv, sub, first=False):
        for mp in range(2):
            part = pv[:, mp * tq:(mp + 1) * tq]
            acc_ref[sub % N_BUF, mp] = part if first else acc_ref[sub % N_BUF, mp] + part

    def bounded_path():
        for sub in range(n_sub):
            load_queries(sub)
            first_block = ratio * (n_sub * step + sub)

            def block_of(r):
                j = first_block + r
                return jnp.where(j >= nk, j - nk, j)

            for r in range(nk):
                slot = r % 2
                j = block_of(r)
                if r > 0:
                    pv = weighted_values(block_of(r - 1), sub, 1 - slot)
                above = 0 if r < ratio else (first_block + r < nk).astype(jnp.int32)
                for mp in range(2):
                    k = ka_ref[mp, pl.ds(block_start(j), tk), :]
                    s = jnp.dot(k, qs_ref[sub % N_BUF, above, mp], preferred_element_type=F32)
                    if r < ratio:
                        s = s - corr_ref[1 + r]
                    p = jnp.exp2(s)
                    psum = jnp.sum(p, axis=0, keepdims=True)
                    l_ref[sub % N_BUF, mp] = psum if r == 0 else l_ref[sub % N_BUF, mp] + psum
                    p_refs[sub % N_BUF][slot][:, mp * tq:(mp + 1) * tq] = p.astype(BF16)
                if r > 0:
                    add_values(pv, sub, first=r == 1)
            add_values(weighted_values(block_of(nk - 1), sub, (nk - 1) % 2), sub)
            finish(sub)

    def online_max_path():
        for sub in range(n_sub):
            load_queries(sub)
            qi = n_sub * step + sub
            m_ref[...] = jnp.full(m_ref.shape, NEG_BIG, F32)
            l_ref[sub % N_BUF] = jnp.zeros(l_ref.shape[1:], F32)
            acc_ref[sub % N_BUF] = jnp.zeros(acc_ref.shape[1:], F32)
            p_refs[sub % N_BUF][1][...] = jnp.zeros(p_refs[sub % N_BUF][1].shape, BF16)

            def scores(j, slot):
                d = j - ratio * qi
                above = (d >= ratio).astype(jnp.int32)
                overlap = jnp.where((d >= 0) & (d < ratio), d + 1, 0)
                for mp in range(2):
                    k = ka_ref[mp, pl.ds(block_start(j), tk), :]
                    s = jnp.dot(k, qs_ref[sub % N_BUF, above, mp], preferred_element_type=F32) - corr_ref[overlap]
                    s_refs[slot][mp] = s
                    bmax_refs[slot][mp] = jnp.max(s, axis=0, keepdims=True)

            def region(j, slot):
                pv = weighted_values(jnp.maximum(j - 1, 0), sub, 1 - slot)
                alphas = []
                for mp in range(2):
                    m_old = m_ref[mp]
                    m_new = jnp.maximum(m_old, bmax_refs[slot][mp])
                    p = jnp.exp2(s_refs[slot][mp] - m_new)
                    alpha = jnp.exp2(m_old - m_new)
                    l_ref[sub % N_BUF, mp] = alpha * l_ref[sub % N_BUF, mp] + jnp.sum(p, axis=0, keepdims=True)
                    m_ref[mp] = m_new
                    p_refs[sub % N_BUF][slot][:, mp * tq:(mp + 1) * tq] = p.astype(BF16)
                    alphas.append(alpha)
                scores(jnp.minimum(j + 1, nk - 1), 1 - slot)
                for mp in range(2):
                    acc_ref[sub % N_BUF, mp] = (acc_ref[sub % N_BUF, mp] + pv[:, mp * tq:(mp + 1) * tq]) * alphas[mp]

            scores(0, 0)

            def pair(i, c):
                region(2 * i, 0)
                region(2 * i + 1, 1)
                return c

            lax.fori_loop(0, nk // 2, pair, 0)
            add_values(weighted_values(nk - 1, sub, 1), sub)
            finish(sub)

    bound2 = jnp.float32(0.0)
    for mp in range(2):
        bound2 = jnp.maximum(bound2, jnp.max(qn2_ref[mp]) * kmax_ref[mp])
    lax.cond(bound2 <= SAFE_LOG2_RANGE ** 2, bounded_path, online_max_path)


def _diffattn(lam_params, qt, ka, vt, qn2, kn2, gd_col, weights, layer, tq, tk, n_sub, lam_init, slopes_l2):
    b, ng, _, s = qt.shape
    n_heads = ng // 2
    tqs = n_sub * tq
    nq = s // tqs
    n_steps = b * n_heads * nq
    bf16_rows = 16

    def slab_rows(rows):
        for k in range(1, rows // bf16_rows + 1):
            if rows % (k * bf16_rows) == 0 and rows // (k * bf16_rows) <= n_steps:
                return k * bf16_rows
        raise ValueError("weight rows do not split into bf16 slabs")

    slabs = [slab_rows(w.shape[1]) for w in weights]

    def slab_index(rows, rps):
        last = rows // rps - 1
        return lambda i, h, q: jnp.minimum((i * n_heads + h) * nq + q, last)

    kern = functools.partial(_diffattn_kernel, tq=tq, tk=tk, n_sub=n_sub, n_cast=len(weights),
                             lam_init=lam_init, slopes_l2=slopes_l2)
    outs = pl.pallas_call(
        kern,
        grid=(b, n_heads, nq),
        in_specs=[
            pl.BlockSpec((4, DIFF_HEAD_DIM), lambda i, h, q: (0, 0)),
            pl.BlockSpec((None, 2, LANES, tqs), lambda i, h, q: (i, h, 0, q)),
            pl.BlockSpec((None, 2, s, LANES), lambda i, h, q: (i, h, 0, 0)),
            pl.BlockSpec((None, None, DIFF_V_DIM, s), lambda i, h, q: (i, h, 0, 0)),
            pl.BlockSpec((None, 2, 1, tqs), lambda i, h, q: (i, h, 0, q)),
            pl.BlockSpec((None, 2, 1, s), lambda i, h, q: (i, h, 0, 0)),
            pl.BlockSpec((DIFF_V_DIM, 1), lambda i, h, q: (0, 0)),
        ] + [
            pl.BlockSpec((None, rps, w.shape[2]),
                         lambda i, h, q, f=slab_index(w.shape[1], rps): (layer, f(i, h, q), 0))
            for w, rps in zip(weights, slabs)
        ],
        out_specs=[pl.BlockSpec((None, tqs, DIFF_V_DIM), lambda i, h, q: (i, q, h))] + [
            pl.BlockSpec((rps, w.shape[2]), lambda i, h, q, f=slab_index(w.shape[1], rps): (f(i, h, q), 0))
            for w, rps in zip(weights, slabs)
        ],
        out_shape=[jax.ShapeDtypeStruct((b, s, n_heads * DIFF_V_DIM), BF16)] + [
            jax.ShapeDtypeStruct(w.shape[1:], BF16) for w in weights
        ],
        scratch_shapes=[
            pltpu.VMEM((N_BUF, 2, 2, LANES, tq), BF16),
            pltpu.VMEM((1 + tq // tk, tk, tq), F32),
            pltpu.VMEM((2, 1, tq), F32),
            pltpu.VMEM((N_BUF, 2, 1, tq), F32),
            pltpu.VMEM((N_BUF, 2, DIFF_V_DIM, tq), F32),
            [pltpu.VMEM((2, tk, tq), F32)] * 2,
            [[pltpu.VMEM((tk, 2 * tq), BF16)] * 2] * N_BUF,
            [pltpu.VMEM((2, 1, tq), F32)] * 2,
            pltpu.SMEM((2,), F32),
        ],
        compiler_params=_cparams(("arbitrary", "arbitrary", "arbitrary")),
        name="diffattn",
    )(lam_params, qt, ka, vt, qn2, kn2, gd_col, *weights)
    return outs[0], outs[1:]


def _retention_kernel(dec_ref, q_ref, k_ref, v_ref, g_ref, gn_ref, o_ref,
                      kvf_ref, kvb_ref, rf_ref, rb_ref, a_ref, *, chunk, group):
    j = pl.program_id(1)
    s_len = q_ref.shape[0]
    n_chunks = s_len // chunk
    hd = RET_HEAD_DIM
    c = chunk

    lane = lax.broadcasted_iota(jnp.int32, (1, LANES), 1)
    rowi = lax.broadcasted_iota(jnp.int32, (LANES, 1), 0)
    first_l = lane < hd
    first_r = rowi < hd

    def per_lane(d):
        return -jnp.exp(jnp.where(first_l, dec_ref[d, 2 * j], dec_ref[d, 2 * j + 1]))

    def per_row(d):
        return -jnp.exp(jnp.where(first_r, dec_ref[d, 2 * j], dec_ref[d, 2 * j + 1]))

    lgf_l, lgb_l = per_lane(0), per_lane(1)
    lgf_r, lgb_r = per_row(0), per_row(1)

    pos_r = lax.broadcasted_iota(jnp.int32, (c, 1), 0).astype(F32)
    wk_f = jnp.exp(lgf_l * (c - 1.0 - pos_r))
    wq_f = jnp.exp(lgf_l * (pos_r + 1.0))
    wk_b = jnp.exp(lgb_l * pos_r)
    wq_b = jnp.exp(lgb_l * (c - pos_r))
    dc_f = jnp.exp(lgf_r * float(c))
    dc_b = jnp.exp(lgb_r * float(c))

    tt = lax.broadcasted_iota(jnp.int32, (c, c), 0)
    ss = lax.broadcasted_iota(jnp.int32, (c, c), 1)
    dist = (tt - ss).astype(F32)

    def dmask(hh):
        lf = -jnp.exp(jnp.full((1, 1), dec_ref[0, 2 * j + hh], F32))
        lb = -jnp.exp(jnp.full((1, 1), dec_ref[1, 2 * j + hh], F32))
        return jnp.where(dist >= 0, jnp.exp(lf * jnp.maximum(dist, 0.0)),
                         jnp.exp(lb * jnp.maximum(-dist, 0.0)))

    kscale = RET_HEAD_DIM ** -0.5
    dcat = jnp.concatenate([dmask(0), dmask(1)], axis=1) * kscale
    wk_f = wk_f * kscale
    wk_b = wk_b * kscale
    wq_f = wq_f.astype(BF16)
    wq_b = wq_b.astype(BF16)
    bd = (first_r == first_l).astype(F32).astype(BF16)
    m0f = first_l.astype(F32)
    m0 = m0f.astype(BF16)
    m1 = (1.0 - m0f).astype(BF16)

    tn = (((0,), (0,)), ((), ()))
    nt = (((1,), (1,)), ((), ()))
    assert n_chunks % group == 0

    def chunk_local(i, carry):
        for u in range(group):
            n = i * group + u
            r0 = pl.multiple_of(n * c, c)
            qb = q_ref[pl.ds(r0, c), :]
            kb = k_ref[pl.ds(r0, c), :]
            vb = v_ref[pl.ds(r0, c), :]
            kf = kb.astype(F32)
            kvf_ref[n] = lax.dot_general((kf * wk_f).astype(BF16), vb, tn, preferred_element_type=F32)
            kvb_ref[n] = lax.dot_general((kf * wk_b).astype(BF16), vb, tn, preferred_element_type=F32)
            kcat = jnp.concatenate([kb * m0, kb * m1], axis=0)
            s = lax.dot_general(qb, kcat, nt, preferred_element_type=F32) * dcat
            a_ref[n] = jnp.concatenate([s.astype(BF16), qb * wq_f, qb * wq_b], axis=1)
        return carry

    lax.fori_loop(0, n_chunks // group, chunk_local, 0)

    def scan(n, carry):
        rf, rb = carry
        nb = n_chunks - 1 - n
        rf_ref[n] = rf.astype(BF16)
        rb_ref[nb] = rb.astype(BF16)
        return dc_f * rf + kvf_ref[n], dc_b * rb + kvb_ref[nb]

    zero = jnp.zeros((LANES, LANES), F32)
    lax.fori_loop(0, n_chunks, scan, (zero, zero))

    gn = gn_ref[...]

    def outputs(i, carry):
        for u in range(group):
            n = i * group + u
            r0 = pl.multiple_of(n * c, c)
            vb = v_ref[pl.ds(r0, c), :]
            bm = jnp.concatenate([vb * m0, vb * m1, rf_ref[n] * bd, rb_ref[n] * bd], axis=0)
            o = jnp.dot(a_ref[n], bm, preferred_element_type=F32)
            o2 = o * o
            ms0 = jnp.sum(o2 * m0f, axis=-1, keepdims=True)
            ms1 = jnp.sum(o2 * (1.0 - m0f), axis=-1, keepdims=True)
            ms = jnp.where(first_l, ms0, ms1) * (1.0 / hd)
            y = o * lax.rsqrt(ms + EPS) * gn
            gate = g_ref[pl.ds(r0, c), :].astype(F32)
            y = y * (gate * jax.nn.sigmoid(gate))
            o_ref[pl.ds(r0, c), :] = y.astype(BF16)
        return carry

    lax.fori_loop(0, n_chunks // group, outputs, 0)


def _retention(dec, proj3, gn_lanes, ret_width, chunk, group=8):
    b, s, _ = proj3.shape
    npair = ret_width // LANES
    assert chunk == LANES
    blk = lambda o: pl.BlockSpec((None, s, LANES), lambda i, j, o=o: (i, 0, o * npair + j))
    return pl.pallas_call(
        functools.partial(_retention_kernel, chunk=chunk, group=group),
        grid=(b, npair),
        in_specs=[
            pl.BlockSpec(memory_space=pltpu.SMEM),
            blk(0), blk(1), blk(2), blk(3),
            pl.BlockSpec((1, LANES), lambda i, j: (0, 0)),
        ],
        out_specs=pl.BlockSpec((None, s, LANES), lambda i, j: (i, 0, j)),
        out_shape=jax.ShapeDtypeStruct((b, s, ret_width), BF16),
        scratch_shapes=(
            [pltpu.VMEM((s // chunk, LANES, LANES), F32)] * 2
            + [pltpu.VMEM((s // chunk, LANES, LANES), BF16)] * 2
            + [pltpu.VMEM((s // chunk, chunk, 4 * chunk), BF16)]
        ),
        compiler_params=_cparams(("parallel", "parallel")),
        name="retention",
    )(dec, proj3, proj3, proj3, proj3, gn_lanes)


def _post_kernel(x_ref, r_ref, a_ref, wo_ref, g_ref, wg_ref, wu_ref, wd_ref, o_ref, *, f_chunks):
    mix = jnp.concatenate([r_ref[...], a_ref[...]], axis=1)
    x1 = x_ref[...] + jnp.dot(mix, wo_ref[...], preferred_element_type=F32)
    h = (x1 * lax.rsqrt(jnp.mean(x1 * x1, axis=-1, keepdims=True) + EPS) * g_ref[...]).astype(BF16)
    acc = x1
    for f0, f1 in f_chunks:
        gate = jnp.dot(h, wg_ref[:, f0:f1], preferred_element_type=F32)
        up = jnp.dot(h, wu_ref[:, f0:f1], preferred_element_type=F32)
        act = (gate * jax.nn.sigmoid(gate) * up).astype(BF16)
        acc = acc + jnp.dot(act, wd_ref[f0:f1, :], preferred_element_type=F32)
    o_ref[...] = acc


def _f_chunks(d_ff, mxu_cols=256, max_cols=1536):
    out, f0 = [], 0
    while f0 < d_ff:
        f1 = min(d_ff, f0 + max_cols)
        out.append((f0, f1))
        f0 = f1
    assert all((a % mxu_cols == 0) for a, _ in out)
    return tuple(out)


def _post(x2, ret2, da2, wo, g, wg, wu, wd, tm):
    m, d = x2.shape
    rw, aw = ret2.shape[1], da2.shape[1]
    d_ff = wg.shape[1]
    weight = lambda rows, cols: pl.BlockSpec((rows, cols), lambda i: (0, 0), pipeline_mode=pl.Buffered(1))
    return pl.pallas_call(
        functools.partial(_post_kernel, f_chunks=_f_chunks(d_ff)),
        grid=(m // tm,),
        in_specs=[
            pl.BlockSpec((tm, d), lambda i: (i, 0)),
            pl.BlockSpec((tm, rw), lambda i: (i, 0)),
            pl.BlockSpec((tm, aw), lambda i: (i, 0)),
            weight(rw + aw, d),
            pl.BlockSpec((1, d), lambda i: (0, 0), pipeline_mode=pl.Buffered(1)),
            weight(d, d_ff),
            weight(d, d_ff),
            weight(d_ff, d),
        ],
        out_specs=pl.BlockSpec((tm, d), lambda i: (i, 0)),
        out_shape=jax.ShapeDtypeStruct((m, d), F32),
        compiler_params=_cparams(("parallel",)),
        name="post",
    )(x2, ret2, da2, wo, g, wg, wu, wd)


def _slope_pieces(n_heads):
    slopes = (2.0 ** (-8.0 * np.arange(1, n_heads + 1, dtype=np.float64) / n_heads) * LOG2E).astype(np.float32)
    cols = np.zeros((n_heads, 2, DIFF_HEAD_DIM, 1), np.float32)
    rem = slopes.astype(np.float64)
    for p in range(N_PIECES):
        piece = rem.astype(BF16).astype(np.float64)
        for dgt in range(N_DIGITS):
            cols[:, 0, N_DIGITS * p + dgt, 0] = piece
            cols[:, 1, N_AUG // 2 + N_DIGITS * p + dgt, 0] = piece
        rem = rem - piece
    return tuple(float(v) for v in slopes), jnp.asarray(cols)


def kernel(x, attn_norm_g, w_in, ret_decay_fwd, ret_decay_bwd, ret_norm_g, dq_norm_g, dk_norm_g,
           lambda_q1, lambda_k1, lambda_q2, lambda_k2, diff_norm_g, w_out, ffn_norm_g,
           w_gate, w_up, w_down):
    b, s, d = x.shape
    depth = w_in.shape[0]
    ret_width = d // 2
    diff_width = d - ret_width
    n_dheads = diff_width // DIFF_V_DIM
    tm = 512
    tq = min(1024, s)
    tk = 512
    n_sub = next(n for n in (4, 2, 1) if s % (n * tq) == 0)
    chunk = 128
    slopes_l2, cpieces = _slope_pieces(n_dheads)

    f32_weights = (w_out, w_gate, w_up, w_down)

    x2 = x.reshape(b * s, d)
    for l in range(depth):
        lam_init = 0.8 - 0.6 * math.exp(-0.3 * l)
        pr, qt, ka, vt, qn2, kn2 = _in_proj(
            x2.reshape(b, s, d), attn_norm_g[l][None], w_in, l, dq_norm_g[l].astype(F32)[:, None],
            dk_norm_g[l].astype(F32)[:, None], cpieces, n_dheads, 4 * ret_width, tm)

        dec = jnp.stack([ret_decay_fwd[l], ret_decay_bwd[l]]).astype(F32)
        gn_lanes = jnp.tile(ret_norm_g[l].astype(F32), LANES // RET_HEAD_DIM)[None]
        ret = _retention(dec, pr, gn_lanes, ret_width, chunk)

        lam_params = jnp.stack([lambda_q1[l], lambda_k1[l], lambda_q2[l], lambda_k2[l]]).astype(F32)
        da, (wo_b, wg_b, wu_b, wd_b) = _diffattn(
            lam_params, qt, ka, vt, qn2, kn2, diff_norm_g[l].astype(F32)[:, None], f32_weights, l,
            tq, tk, n_sub, lam_init, slopes_l2)

        x2 = _post(x2, ret.reshape(b * s, ret_width), da.reshape(b * s, diff_width),
                   wo_b, ffn_norm_g[l][None], wg_b, wu_b, wd_b, tm)
    return x2.reshape(b, s, d)
```

```python
import functools
import math

import numpy as np
import jax
import jax.numpy as jnp
from jax import lax
from jax.experimental import pallas as pl
from jax.experimental.pallas import tpu as pltpu

EPS = 1e-6
LOG2E = 1.4426950408889634

RET_HEAD_DIM = 64
DIFF_HEAD_DIM = 64
DIFF_V_DIM = 128
LANES = 128
VMEM_LIMIT = 56 * 1024 * 1024

BF16 = jnp.bfloat16
F32 = jnp.float32


def _cparams(sem):
    return pltpu.CompilerParams(dimension_semantics=sem, vmem_limit_bytes=VMEM_LIMIT)


N_DIGITS = 3
N_PIECES = 3
N_AUG = 2 * N_DIGITS * N_PIECES
POS_RADIX = 32
CAST_COLS = 512


def _pos_digits(pos):
    d0 = pos % POS_RADIX
    d1 = (pos // POS_RADIX) % POS_RADIX * POS_RADIX
    d2 = pos // (POS_RADIX * POS_RADIX) * (POS_RADIX * POS_RADIX)
    return [d.astype(F32) for d in (d0, d1, d2)]


def _in_proj_kernel(x_ref, g_ref, w_ref, gq_ref, gk_ref, cp_ref,
                    pr_ref, qt_ref, ka_ref, vt_ref, qn2_ref, kn2_ref, wb_ref, *, n_heads, ret_cols):
    ts = x_ref.shape[0]
    ng = 2 * n_heads
    hd = DIFF_HEAD_DIM
    half = N_AUG // 2
    aw = n_heads * DIFF_V_DIM

    @pl.when((pl.program_id(0) == 0) & (pl.program_id(1) == 0))
    def _():
        for c0 in range(0, w_ref.shape[1], CAST_COLS):
            wb_ref[:, c0:c0 + CAST_COLS] = w_ref[:, c0:c0 + CAST_COLS].astype(BF16)

    x = x_ref[...]
    y = (x * lax.rsqrt(jnp.mean(x * x, axis=-1, keepdims=True) + EPS) * g_ref[...]).astype(BF16)
    dqkv = jnp.dot(y, wb_ref[:, ret_cols:], preferred_element_type=F32)
    pr_ref[...] = jnp.dot(y, wb_ref[:, :ret_cols], preferred_element_type=F32).astype(BF16)

    def head_norm_t(xf, g_col, scale):
        xt = xf.T.reshape(ng, hd, ts)
        ms = jnp.mean(xt * xt, axis=1, keepdims=True)
        return (xt * lax.rsqrt(ms + EPS) * (g_col[None] * scale)).astype(BF16)

    qn = head_norm_t(dqkv[:, :aw], gq_ref[...], DIFF_HEAD_DIM ** -0.5 * LOG2E)
    kn = head_norm_t(dqkv[:, aw:2 * aw], gk_ref[...], 1.0)
    qn2_ref[...] = jnp.sum(jnp.square(qn.astype(F32)), axis=1, keepdims=True)
    kn2_ref[...] = jnp.sum(jnp.square(kn.astype(F32)), axis=1, keepdims=True)

    pos = pl.program_id(1) * ts + lax.broadcasted_iota(jnp.int32, (1, ts), 1)
    d0, d1, d2 = _pos_digits(pos)
    r = lax.broadcasted_iota(jnp.int32, (hd, 1), 0)
    which = jnp.where(r < half, r, r - half) % N_DIGITS
    digit = jnp.where(which == 0, d0, jnp.where(which == 1, d1, d2))
    k_digits = jnp.where(r < half, digit, 0.0)
    q_digits = jnp.where((r >= half) & (r < N_AUG), -digit, 0.0)
    for g in range(ng):
        q_rows = (q_digits + cp_ref[g // 2, 0]).astype(BF16)
        k_rows = (k_digits + cp_ref[g // 2, 1]).astype(BF16)
        qt_ref[g] = jnp.concatenate([qn[g], q_rows], axis=0)
        ka_ref[g] = jnp.concatenate([kn[g], k_rows], axis=0).astype(F32).T.astype(BF16)
    vt_ref[...] = dqkv[:, 2 * aw:].T.reshape(n_heads, DIFF_V_DIM, ts).astype(BF16)


def _in_proj(x3, g, w_all, layer, gq, gk, cpieces, n_heads, ret_cols, ts):
    b, s, d = x3.shape
    n = w_all.shape[2]
    ng = 2 * n_heads
    assert n == ret_cols + 3 * n_heads * DIFF_V_DIM and n % CAST_COLS == 0
    const = lambda shape: pl.BlockSpec(shape, lambda i, j: (0,) * len(shape), pipeline_mode=pl.Buffered(1))
    return pl.pallas_call(
        functools.partial(_in_proj_kernel, n_heads=n_heads, ret_cols=ret_cols),
        grid=(b, s // ts),
        in_specs=[
            pl.BlockSpec((None, ts, d), lambda i, j: (i, j, 0)),
            const((1, d)),
            pl.BlockSpec((None, d, n), lambda i, j: (layer, 0, 0), pipeline_mode=pl.Buffered(1)),
            const((DIFF_HEAD_DIM, 1)),
            const((DIFF_HEAD_DIM, 1)),
            const((n_heads, 2, DIFF_HEAD_DIM, 1)),
        ],
        out_specs=[
            pl.BlockSpec((None, ts, ret_cols), lambda i, j: (i, j, 0)),
            pl.BlockSpec((None, ng, LANES, ts), lambda i, j: (i, 0, 0, j)),
            pl.BlockSpec((None, ng, ts, LANES), lambda i, j: (i, 0, j, 0)),
            pl.BlockSpec((None, n_heads, DIFF_V_DIM, ts), lambda i, j: (i, 0, 0, j)),
            pl.BlockSpec((None, ng, 1, ts), lambda i, j: (i, 0, 0, j)),
            pl.BlockSpec((None, ng, 1, ts), lambda i, j: (i, 0, 0, j)),
        ],
        out_shape=[
            jax.ShapeDtypeStruct((b, s, ret_cols), BF16),
            jax.ShapeDtypeStruct((b, ng, LANES, s), BF16),
            jax.ShapeDtypeStruct((b, ng, s, LANES), BF16),
            jax.ShapeDtypeStruct((b, n_heads, DIFF_V_DIM, s), BF16),
            jax.ShapeDtypeStruct((b, ng, 1, s), F32),
            jax.ShapeDtypeStruct((b, ng, 1, s), F32),
        ],
        scratch_shapes=[pltpu.VMEM((d, n), BF16)],
        compiler_params=_cparams(("arbitrary", "arbitrary")),
        name="in_proj",
    )(x3, g, w_all, gq, gk, cpieces)


NEG_BIG = -1e30
SAFE_LOG2_RANGE = 96.0


def _diffattn_kernel(lam_ref, qt_ref, ka_ref, vt_ref, qn2_ref, kn2_ref, gd_ref, *rest,
                     tq, tk, n_sub, n_cast, lam_init, slopes_l2):
    cast_in, o_ref, cast_out = rest[:n_cast], rest[n_cast], rest[n_cast + 1:2 * n_cast + 1]
    qs_ref, corr_ref, m_ref, l_ref, acc_ref, s_refs, p_refs, bmax_refs, kmax_ref = rest[2 * n_cast + 1:]
    def convert_weights():
        for src, dst in zip(cast_in, cast_out):
            dst[...] = src[...].astype(BF16)

    _diffattn_body(lam_ref, qt_ref, ka_ref, vt_ref, qn2_ref, kn2_ref, gd_ref, o_ref,
                   qs_ref, corr_ref, m_ref, l_ref, acc_ref, s_refs, p_refs, bmax_refs, kmax_ref,
                   convert_weights, tq=tq, tk=tk, n_sub=n_sub, lam_init=lam_init, slopes_l2=slopes_l2)


def _diffattn_body(lam_ref, qt_ref, ka_ref, vt_ref, qn2_ref, kn2_ref, gd_ref, o_ref,
                   qs_ref, corr_ref, m_ref, l_ref, acc_ref, s_refs, p_refs, bmax_refs, kmax_ref,
                   side_work, *, tq, tk, n_sub, lam_init, slopes_l2):
    h = pl.program_id(1)
    step = pl.program_id(2)
    s_len = ka_ref.shape[1]
    nk = s_len // tk
    ratio = tq // tk
    assert tq % tk == 0 and nk % 2 == 0
    slope = jnp.float32(0.0)
    for hh, sv in enumerate(slopes_l2):
        slope = jnp.where(h == hh, jnp.float32(sv), slope)

    @pl.when(step == 0)
    def _():
        ii = lax.broadcasted_iota(jnp.int32, (tk, tq), 0)
        jj = lax.broadcasted_iota(jnp.int32, (tk, tq), 1)
        corr_ref[0] = jnp.zeros((tk, tq), F32)
        for d in range(ratio):
            corr_ref[1 + d] = (2.0 * slope) * jnp.maximum(ii - jj + d * tk, 0).astype(F32)
        for mp in range(2):
            kmax_ref[mp] = jnp.max(kn2_ref[mp])

    la = lam_ref[...]
    lam = (jnp.exp(jnp.sum(la[0:1] * la[1:2], axis=-1, keepdims=True))
           - jnp.exp(jnp.sum(la[2:3] * la[3:4], axis=-1, keepdims=True)) + lam_init)
    out_gain = gd_ref[...] * (1.0 - lam_init)
    row = lax.broadcasted_iota(jnp.int32, (LANES, 1), 0)
    sign = jnp.where(row < DIFF_HEAD_DIM, 1.0, -1.0).astype(BF16)

    def load_queries(sub):
        for mp in range(2):
            q = qt_ref[mp, :, sub * tq:(sub + 1) * tq]
            qs_ref[sub, 0, mp] = q
            qs_ref[sub, 1, mp] = q * sign

    def finish(sub):
        o1 = acc_ref[sub, 0] * (1.0 / l_ref[sub, 0])
        o2 = acc_ref[sub, 1] * (lam / l_ref[sub, 1])
        da = o1 - o2
        y = da * lax.rsqrt(jnp.mean(da * da, axis=0, keepdims=True) + EPS) * out_gain
        o_ref[sub * tq:(sub + 1) * tq, :] = y.T.astype(BF16)

    def block_start(j):
        return pl.multiple_of(j * tk, tk)

    def weighted_values(j, sub, slot):
        return jnp.dot(vt_ref[:, pl.ds(block_start(j), tk)], p_refs[sub][slot][...],
                       preferred_element_type=F32)

    def add_values(pv, sub, first=False):
        for mp in range(2):
            part = pv[:, mp * tq:(mp + 1) * tq]
            acc_ref[sub, mp] = part if first else acc_ref[sub, mp] + part

    def bounded_path():
        side_work()
        for sub in range(n_sub):
            load_queries(sub)
            first_block = ratio * (n_sub * step + sub)

            def block_of(r):
                j = first_block + r
                return jnp.where(j >= nk, j - nk, j)

            for r in range(nk):
                slot = r % 2
                j = block_of(r)
                if r > 0:
                    pv = weighted_values(block_of(r - 1), sub, 1 - slot)
                above = 0 if r < ratio else (first_block + r < nk).astype(jnp.int32)
                for mp in range(2):
                    k = ka_ref[mp, pl.ds(block_start(j), tk), :]
                    s = jnp.dot(k, qs_ref[sub, above, mp], preferred_element_type=F32)
                    if r < ratio:
                        s = s - corr_ref[1 + r]
                    p = jnp.exp2(s)
                    psum = jnp.sum(p, axis=0, keepdims=True)
                    l_ref[sub, mp] = psum if r == 0 else l_ref[sub, mp] + psum
                    p_refs[sub][slot][:, mp * tq:(mp + 1) * tq] = p.astype(BF16)
                if r > 0:
                    add_values(pv, sub, first=r == 1)
            add_values(weighted_values(block_of(nk - 1), sub, (nk - 1) % 2), sub)
            finish(sub)

    def online_max_path():
        side_work()
        for sub in range(n_sub):
            load_queries(sub)
            qi = n_sub * step + sub
            m_ref[...] = jnp.full(m_ref.shape, NEG_BIG, F32)
            l_ref[sub] = jnp.zeros(l_ref.shape[1:], F32)
            acc_ref[sub] = jnp.zeros(acc_ref.shape[1:], F32)
            p_refs[sub][1][...] = jnp.zeros(p_refs[sub][1].shape, BF16)

            def scores(j, slot):
                d = j - ratio * qi
                above = (d >= ratio).astype(jnp.int32)
                overlap = jnp.where((d >= 0) & (d < ratio), d + 1, 0)
                for mp in range(2):
                    k = ka_ref[mp, pl.ds(block_start(j), tk), :]
                    s = jnp.dot(k, qs_ref[sub, above, mp], preferred_element_type=F32) - corr_ref[overlap]
                    s_refs[slot][mp] = s
                    bmax_refs[slot][mp] = jnp.max(s, axis=0, keepdims=True)

            def region(j, slot):
                pv = weighted_values(jnp.maximum(j - 1, 0), sub, 1 - slot)
                alphas = []
                for mp in range(2):
                    m_old = m_ref[mp]
                    m_new = jnp.maximum(m_old, bmax_refs[slot][mp])
                    p = jnp.exp2(s_refs[slot][mp] - m_new)
                    alpha = jnp.exp2(m_old - m_new)
                    l_ref[sub, mp] = alpha * l_ref[sub, mp] + jnp.sum(p, axis=0, keepdims=True)
                    m_ref[mp] = m_new
                    p_refs[sub][slot][:, mp * tq:(mp + 1) * tq] = p.astype(BF16)
                    alphas.append(alpha)
                scores(jnp.minimum(j + 1, nk - 1), 1 - slot)
                for mp in range(2):
                    acc_ref[sub, mp] = (acc_ref[sub, mp] + pv[:, mp * tq:(mp + 1) * tq]) * alphas[mp]

            scores(0, 0)

            def pair(i, c):
                region(2 * i, 0)
                region(2 * i + 1, 1)
                return c

            lax.fori_loop(0, nk // 2, pair, 0)
            add_values(weighted_values(nk - 1, sub, 1), sub)
            finish(sub)

    bound2 = jnp.float32(0.0)
    for mp in range(2):
        bound2 = jnp.maximum(bound2, jnp.max(qn2_ref[mp]) * kmax_ref[mp])
    lax.cond(bound2 <= SAFE_LOG2_RANGE ** 2, bounded_path, online_max_path)


def _diffattn(lam_params, qt, ka, vt, qn2, kn2, gd_col, weights, layer, tq, tk, n_sub, lam_init, slopes_l2):
    b, ng, _, s = qt.shape
    n_heads = ng // 2
    tqs = n_sub * tq
    nq = s // tqs
    n_steps = b * n_heads * nq
    bf16_rows = 16

    def slab_rows(rows):
        for k in range(1, rows // bf16_rows + 1):
            if rows % (k * bf16_rows) == 0 and rows // (k * bf16_rows) <= n_steps:
                return k * bf16_rows
        raise ValueError("weight rows do not split into bf16 slabs")

    slabs = [slab_rows(w.shape[1]) for w in weights]

    def slab_index(rows, rps):
        last = rows // rps - 1
        return lambda i, h, q: jnp.minimum((i * n_heads + h) * nq + q, last)

    kern = functools.partial(_diffattn_kernel, tq=tq, tk=tk, n_sub=n_sub, n_cast=len(weights),
                             lam_init=lam_init, slopes_l2=slopes_l2)
    outs = pl.pallas_call(
        kern,
        grid=(b, n_heads, nq),
        in_specs=[
            pl.BlockSpec((4, DIFF_HEAD_DIM), lambda i, h, q: (0, 0)),
            pl.BlockSpec((None, 2, LANES, tqs), lambda i, h, q: (i, h, 0, q)),
            pl.BlockSpec((None, 2, s, LANES), lambda i, h, q: (i, h, 0, 0)),
            pl.BlockSpec((None, None, DIFF_V_DIM, s), lambda i, h, q: (i, h, 0, 0)),
            pl.BlockSpec((None, 2, 1, tqs), lambda i, h, q: (i, h, 0, q)),
            pl.BlockSpec((None, 2, 1, s), lambda i, h, q: (i, h, 0, 0)),
            pl.BlockSpec((DIFF_V_DIM, 1), lambda i, h, q: (0, 0)),
        ] + [
            pl.BlockSpec((None, rps, w.shape[2]),
                         lambda i, h, q, f=slab_index(w.shape[1], rps): (layer, f(i, h, q), 0))
            for w, rps in zip(weights, slabs)
        ],
        out_specs=[pl.BlockSpec((None, tqs, DIFF_V_DIM), lambda i, h, q: (i, q, h))] + [
            pl.BlockSpec((rps, w.shape[2]), lambda i, h, q, f=slab_index(w.shape[1], rps): (f(i, h, q), 0))
            for w, rps in zip(weights, slabs)
        ],
        out_shape=[jax.ShapeDtypeStruct((b, s, n_heads * DIFF_V_DIM), BF16)] + [
            jax.ShapeDtypeStruct(w.shape[1:], BF16) for w in weights
        ],
        scratch_shapes=[
            pltpu.VMEM((n_sub, 2, 2, LANES, tq), BF16),
            pltpu.VMEM((1 + tq // tk, tk, tq), F32),
            pltpu.VMEM((2, 1, tq), F32),
            pltpu.VMEM((n_sub, 2, 1, tq), F32),
            pltpu.VMEM((n_sub, 2, DIFF_V_DIM, tq), F32),
            [pltpu.VMEM((2, tk, tq), F32)] * 2,
            [[pltpu.VMEM((tk, 2 * tq), BF16)] * 2] * n_sub,
            [pltpu.VMEM((2, 1, tq), F32)] * 2,
            pltpu.SMEM((2,), F32),
        ],
        compiler_params=_cparams(("arbitrary", "arbitrary", "arbitrary")),
        name="diffattn",
    )(lam_params, qt, ka, vt, qn2, kn2, gd_col, *weights)
    return outs[0], outs[1:]


def _retention_kernel(dec_ref, q_ref, k_ref, v_ref, g_ref, gn_ref, o_ref,
                      kvf_ref, kvb_ref, rf_ref, rb_ref, a_ref, *, chunk, group):
    j = pl.program_id(1)
    s_len = q_ref.shape[0]
    n_chunks = s_len // chunk
    hd = RET_HEAD_DIM
    c = chunk

    lane = lax.broadcasted_iota(jnp.int32, (1, LANES), 1)
    rowi = lax.broadcasted_iota(jnp.int32, (LANES, 1), 0)
    first_l = lane < hd
    first_r = rowi < hd

    def per_lane(d):
        return -jnp.exp(jnp.where(first_l, dec_ref[d, 2 * j], dec_ref[d, 2 * j + 1]))

    def per_row(d):
        return -jnp.exp(jnp.where(first_r, dec_ref[d, 2 * j], dec_ref[d, 2 * j + 1]))

    lgf_l, lgb_l = per_lane(0), per_lane(1)
    lgf_r, lgb_r = per_row(0), per_row(1)

    pos_r = lax.broadcasted_iota(jnp.int32, (c, 1), 0).astype(F32)
    wk_f = jnp.exp(lgf_l * (c - 1.0 - pos_r))
    wq_f = jnp.exp(lgf_l * (pos_r + 1.0))
    wk_b = jnp.exp(lgb_l * pos_r)
    wq_b = jnp.exp(lgb_l * (c - pos_r))
    dc_f = jnp.exp(lgf_r * float(c))
    dc_b = jnp.exp(lgb_r * float(c))

    tt = lax.broadcasted_iota(jnp.int32, (c, c), 0)
    ss = lax.broadcasted_iota(jnp.int32, (c, c), 1)
    dist = (tt - ss).astype(F32)

    def dmask(hh):
        lf = -jnp.exp(jnp.full((1, 1), dec_ref[0, 2 * j + hh], F32))
        lb = -jnp.exp(jnp.full((1, 1), dec_ref[1, 2 * j + hh], F32))
        return jnp.where(dist >= 0, jnp.exp(lf * jnp.maximum(dist, 0.0)),
                         jnp.exp(lb * jnp.maximum(-dist, 0.0)))

    kscale = RET_HEAD_DIM ** -0.5
    dcat = jnp.concatenate([dmask(0), dmask(1)], axis=1) * kscale
    wk_f = wk_f * kscale
    wk_b = wk_b * kscale
    wq_f = wq_f.astype(BF16)
    wq_b = wq_b.astype(BF16)
    bd = (first_r == first_l).astype(F32).astype(BF16)
    m0f = first_l.astype(F32)
    m0 = m0f.astype(BF16)
    m1 = (1.0 - m0f).astype(BF16)

    tn = (((0,), (0,)), ((), ()))
    nt = (((1,), (1,)), ((), ()))
    assert n_chunks % group == 0

    def chunk_local(i, carry):
        for u in range(group):
            n = i * group + u
            r0 = pl.multiple_of(n * c, c)
            qb = q_ref[pl.ds(r0, c), :]
            kb = k_ref[pl.ds(r0, c), :]
            vb = v_ref[pl.ds(r0, c), :]
            kf = kb.astype(F32)
            kvf_ref[n] = lax.dot_general((kf * wk_f).astype(BF16), vb, tn, preferred_element_type=F32)
            kvb_ref[n] = lax.dot_general((kf * wk_b).astype(BF16), vb, tn, preferred_element_type=F32)
            kcat = jnp.concatenate([kb * m0, kb * m1], axis=0)
            s = lax.dot_general(qb, kcat, nt, preferred_element_type=F32) * dcat
            a_ref[n] = jnp.concatenate([s.astype(BF16), qb * wq_f, qb * wq_b], axis=1)
        return carry

    lax.fori_loop(0, n_chunks // group, chunk_local, 0)

    def scan(n, carry):
        rf, rb = carry
        nb = n_chunks - 1 - n
        rf_ref[n] = rf.astype(BF16)
        rb_ref[nb] = rb.astype(BF16)
        return dc_f * rf + kvf_ref[n], dc_b * rb + kvb_ref[nb]

    zero = jnp.zeros((LANES, LANES), F32)
    lax.fori_loop(0, n_chunks, scan, (zero, zero))

    gn = gn_ref[...]

    def outputs(i, carry):
        for u in range(group):
            n = i * group + u
            r0 = pl.multiple_of(n * c, c)
            vb = v_ref[pl.ds(r0, c), :]
            bm = jnp.concatenate([vb * m0, vb * m1, rf_ref[n] * bd, rb_ref[n] * bd], axis=0)
            o = jnp.dot(a_ref[n], bm, preferred_element_type=F32)
            o2 = o * o
            ms0 = jnp.sum(o2 * m0f, axis=-1, keepdims=True)
            ms1 = jnp.sum(o2 * (1.0 - m0f), axis=-1, keepdims=True)
            ms = jnp.where(first_l, ms0, ms1) * (1.0 / hd)
            y = o * lax.rsqrt(ms + EPS) * gn
            gate = g_ref[pl.ds(r0, c), :].astype(F32)
            y = y * (gate * jax.nn.sigmoid(gate))
            o_ref[pl.ds(r0, c), :] = y.astype(BF16)
        return carry

    lax.fori_loop(0, n_chunks // group, outputs, 0)


def _retention(dec, proj3, gn_lanes, ret_width, chunk, group=8):
    b, s, _ = proj3.shape
    npair = ret_width // LANES
    assert chunk == LANES
    blk = lambda o: pl.BlockSpec((None, s, LANES), lambda i, j, o=o: (i, 0, o * npair + j))
    return pl.pallas_call(
        functools.partial(_retention_kernel, chunk=chunk, group=group),
        grid=(b, npair),
        in_specs=[
            pl.BlockSpec(memory_space=pltpu.SMEM),
            blk(0), blk(1), blk(2), blk(3),
            pl.BlockSpec((1, LANES), lambda i, j: (0, 0)),
        ],
        out_specs=pl.BlockSpec((None, s, LANES), lambda i, j: (i, 0, j)),
        out_shape=jax.ShapeDtypeStruct((b, s, ret_width), BF16),
        scratch_shapes=(
            [pltpu.VMEM((s // chunk, LANES, LANES), F32)] * 2
            + [pltpu.VMEM((s // chunk, LANES, LANES), BF16)] * 2
            + [pltpu.VMEM((s // chunk, chunk, 4 * chunk), BF16)]
        ),
        compiler_params=_cparams(("parallel", "parallel")),
        name="retention",
    )(dec, proj3, proj3, proj3, proj3, gn_lanes)


def _post_kernel(x_ref, r_ref, a_ref, wo_ref, g_ref, wg_ref, wu_ref, wd_ref, o_ref, *, f_chunks):
    mix = jnp.concatenate([r_ref[...], a_ref[...]], axis=1)
    x1 = x_ref[...] + jnp.dot(mix, wo_ref[...], preferred_element_type=F32)
    h = (x1 * lax.rsqrt(jnp.mean(x1 * x1, axis=-1, keepdims=True) + EPS) * g_ref[...]).astype(BF16)
    acc = x1
    for f0, f1 in f_chunks:
        gate = jnp.dot(h, wg_ref[:, f0:f1], preferred_element_type=F32)
        up = jnp.dot(h, wu_ref[:, f0:f1], preferred_element_type=F32)
        act = (gate * jax.nn.sigmoid(gate) * up).astype(BF16)
        acc = acc + jnp.dot(act, wd_ref[f0:f1, :], preferred_element_type=F32)
    o_ref[...] = acc


def _f_chunks(d_ff, mxu_cols=256, max_cols=1536):
    out, f0 = [], 0
    while f0 < d_ff:
        f1 = min(d_ff, f0 + max_cols)
        out.append((f0, f1))
        f0 = f1
    assert all((a % mxu_cols == 0) for a, _ in out)
    return tuple(out)


def _post(x2, ret2, da2, wo, g, wg, wu, wd, tm):
    m, d = x2.shape
    rw, aw = ret2.shape[1], da2.shape[1]
    d_ff = wg.shape[1]
    weight = lambda rows, cols: pl.BlockSpec((rows, cols), lambda i: (0, 0), pipeline_mode=pl.Buffered(1))
    return pl.pallas_call(
        functools.partial(_post_kernel, f_chunks=_f_chunks(d_ff)),
        grid=(m // tm,),
        in_specs=[
            pl.BlockSpec((tm, d), lambda i: (i, 0)),
            pl.BlockSpec((tm, rw), lambda i: (i, 0)),
            pl.BlockSpec((tm, aw), lambda i: (i, 0)),
            weight(rw + aw, d),
            pl.BlockSpec((1, d), lambda i: (0, 0), pipeline_mode=pl.Buffered(1)),
            weight(d, d_ff),
            weight(d, d_ff),
            weight(d_ff, d),
        ],
        out_specs=pl.BlockSpec((tm, d), lambda i: (i, 0)),
        out_shape=jax.ShapeDtypeStruct((m, d), F32),
        compiler_params=_cparams(("parallel",)),
        name="post",
    )(x2, ret2, da2, wo, g, wg, wu, wd)


def _slope_pieces(n_heads):
    slopes = (2.0 ** (-8.0 * np.arange(1, n_heads + 1, dtype=np.float64) / n_heads) * LOG2E).astype(np.float32)
    cols = np.zeros((n_heads, 2, DIFF_HEAD_DIM, 1), np.float32)
    rem = slopes.astype(np.float64)
    for p in range(N_PIECES):
        piece = rem.astype(BF16).astype(np.float64)
        for dgt in range(N_DIGITS):
            cols[:, 0, N_DIGITS * p + dgt, 0] = piece
            cols[:, 1, N_AUG // 2 + N_DIGITS * p + dgt, 0] = piece
        rem = rem - piece
    return tuple(float(v) for v in slopes), jnp.asarray(cols)


def kernel(x, attn_norm_g, w_in, ret_decay_fwd, ret_decay_bwd, ret_norm_g, dq_norm_g, dk_norm_g,
           lambda_q1, lambda_k1, lambda_q2, lambda_k2, diff_norm_g, w_out, ffn_norm_g,
           w_gate, w_up, w_down):
    b, s, d = x.shape
    depth = w_in.shape[0]
    ret_width = d // 2
    diff_width = d - ret_width
    n_dheads = diff_width // DIFF_V_DIM
    tm = 512
    tq = min(1024, s)
    tk = 512
    n_sub = 2 if s % (2 * tq) == 0 else 1
    chunk = 128
    slopes_l2, cpieces = _slope_pieces(n_dheads)

    f32_weights = (w_out, w_gate, w_up, w_down)

    x2 = x.reshape(b * s, d)
    for l in range(depth):
        lam_init = 0.8 - 0.6 * math.exp(-0.3 * l)
        pr, qt, ka, vt, qn2, kn2 = _in_proj(
            x2.reshape(b, s, d), attn_norm_g[l][None], w_in, l, dq_norm_g[l].astype(F32)[:, None],
            dk_norm_g[l].astype(F32)[:, None], cpieces, n_dheads, 4 * ret_width, tm)

        dec = jnp.stack([ret_decay_fwd[l], ret_decay_bwd[l]]).astype(F32)
        gn_lanes = jnp.tile(ret_norm_g[l].astype(F32), LANES // RET_HEAD_DIM)[None]
        ret = _retention(dec, pr, gn_lanes, ret_width, chunk)

        lam_params = jnp.stack([lambda_q1[l], lambda_k1[l], lambda_q2[l], lambda_k2[l]]).astype(F32)
        da, (wo_b, wg_b, wu_b, wd_b) = _diffattn(
            lam_params, qt, ka, vt, qn2, kn2, diff_norm_g[l].astype(F32)[:, None], f32_weights, l,
            tq, tk, n_sub, lam_init, slopes_l2)

        x2 = _post(x2, ret.reshape(b * s, ret_width), da.reshape(b * s, diff_width),
                   wo_b, ffn_norm_g[l][None], wg_b, wu_b, wd_b, tm)
    return x2.reshape(b, s, d)
```
